```python
import math
import jax
import jax.numpy as jnp
from jax import lax
import numpy as np

D_MODEL = 1024
BATCH = 32
SEQ = 2048
DEPTH = 4

GRID_W = 64
CTX_LEN = 256
N_MIXERS = 2
D_FF = 4 * D_MODEL
NORM_EPS = 1e-6
ROPE_THETA = 10000.0
DN_HEAD_DIM = 128
DN_HEADS = D_MODEL // DN_HEAD_DIM
DN_WIDTH = DN_HEADS * DN_HEAD_DIM
DN_PROJ = 4 * DN_WIDTH + 4 * DN_HEADS
DN_CONV_W = 5
DN_CHUNK = 64
DA_HEAD_DIM = 64
DA_HEADS = D_MODEL // (2 * DA_HEAD_DIM)
DA_V_DIM = 2 * DA_HEAD_DIM
DA_QK_WIDTH = DA_HEADS * 2 * DA_HEAD_DIM
DA_V_WIDTH = DA_HEADS * DA_V_DIM
DA_PROJ = 2 * DA_QK_WIDTH + DA_V_WIDTH
Q_BLOCK = 128
N_DN_LAYERS = (DEPTH + 1) // 2
N_DA_LAYERS = DEPTH // 2

kernel_name = 'hybrid_deltanet_diffattn_dit_block'


def rms_norm(x, w):
    xf = x.astype(jnp.float32)
    y = xf * lax.rsqrt(jnp.mean(xf * xf, axis=-1, keepdims=True) + NORM_EPS)
    return (y * w.astype(jnp.float32)).astype(x.dtype)


def modulate(x, w, shift, scale):
    return rms_norm(x, w) * (1 + scale) + shift


def squared_relu_mlp(h, w1, w2):
    a = jax.nn.relu(h @ w1)
    return (a * a) @ w2


def axial_rope(n_tokens, dim):
    rows = n_tokens // GRID_W
    row = jnp.repeat(jnp.arange(rows, dtype=jnp.int32), GRID_W).astype(jnp.float32)
    col = jnp.tile(jnp.arange(GRID_W, dtype=jnp.int32), rows).astype(jnp.float32)
    n_freq = dim // 4
    inv = ROPE_THETA ** (-jnp.arange(n_freq, dtype=jnp.float32) / n_freq)
    ang = jnp.concatenate([row[:, None] * inv, col[:, None] * inv], axis=-1)
    return jnp.cos(ang), jnp.sin(ang)


def apply_rope(x, cos, sin):
    half = x.shape[-1] // 2
    x1, x2 = x[..., :half], x[..., half:]
    cos = cos[:, None, None, :].astype(x.dtype)
    sin = sin[:, None, None, :].astype(x.dtype)
    return jnp.concatenate([x1 * cos - x2 * sin, x1 * sin + x2 * cos], axis=-1)


def short_conv(x, w):
    pad = (DN_CONV_W - 1) // 2
    y = lax.conv_general_dilated(x, w[:, None, :].astype(x.dtype), window_strides=(1,),
                                 padding=[(pad, pad)], dimension_numbers=('NWC', 'WIO', 'NWC'),
                                 feature_group_count=x.shape[-1])
    return jax.nn.silu(y)


def l2_normalize(x):
    xf = x.astype(jnp.float32)
    return xf * lax.rsqrt(jnp.sum(xf * xf, axis=-1, keepdims=True) + 1e-6)


def gated_delta_chunked(q, k, v, g, beta, s0):
    b, h, t, dk = q.shape
    dv = v.shape[-1]
    n = t // DN_CHUNK
    q = q.reshape(b, h, n, DN_CHUNK, dk)
    k = k.reshape(b, h, n, DN_CHUNK, dk)
    v = v.reshape(b, h, n, DN_CHUNK, dv)
    g = g.reshape(b, h, n, DN_CHUNK)
    beta = beta.reshape(b, h, n, DN_CHUNK)
    G = jnp.cumsum(g, axis=-1)
    idx = jnp.arange(DN_CHUNK)
    incl = idx[:, None] >= idx[None, :]
    strict = idx[:, None] > idx[None, :]
    decay = jnp.exp(jnp.where(incl, G[..., :, None] - G[..., None, :], -jnp.inf))
    kk = jnp.einsum('bhnid,bhnjd->bhnij', k, k)
    a_mat = jnp.where(strict, kk * decay * beta[..., :, None], 0.0)
    unit_lower = a_mat + jnp.eye(DN_CHUNK, dtype=q.dtype)
    rhs = jnp.concatenate([v * beta[..., None], k * (beta * jnp.exp(G))[..., None]], axis=-1)
    sol = lax.linalg.triangular_solve(unit_lower, rhs, left_side=True, lower=True, unit_diagonal=True)
    u, w = sol[..., :dv], sol[..., dv:]
    qk = jnp.einsum('bhnid,bhnjd->bhnij', q, k) * decay
    q_dec = q * jnp.exp(G)[..., None]
    k_dec = k * jnp.exp(G[..., -1:] - G)[..., None]
    g_tot = jnp.exp(G[..., -1])

    def step(state, xs):
        u_c, w_c, qk_c, qd_c, kd_c, gt_c = xs
        nu = u_c - jnp.einsum('bhcd,bhde->bhce', w_c, state)
        out = jnp.einsum('bhcd,bhde->bhce', qd_c, state) + jnp.einsum('bhij,bhje->bhie', qk_c, nu)
        state = state * gt_c[..., None, None] + jnp.einsum('bhcd,bhce->bhde', kd_c, nu)
        return state, out

    xs = (jnp.moveaxis(u, 2, 0), jnp.moveaxis(w, 2, 0), jnp.moveaxis(qk, 2, 0),
          jnp.moveaxis(q_dec, 2, 0), jnp.moveaxis(k_dec, 2, 0), jnp.moveaxis(g_tot, 2, 0))
    s_final, o = lax.scan(step, s0, xs)
    o = jnp.moveaxis(o, 0, 2).reshape(b, h, t, dv)
    return o, s_final


def deltanet_mixer(h_lat, h_ctx, w_in, conv_w, a_log, dt_bias, out_norm, w_out, with_ctx_out):
    def prep(hs):
        bsz, t, _ = hs.shape
        proj = hs @ w_in
        qkv = short_conv(proj[..., :3 * DN_WIDTH], conv_w)
        z = proj[..., 3 * DN_WIDTH:4 * DN_WIDTH]
        gate_in = proj[..., 4 * DN_WIDTH:].astype(jnp.float32)

        def heads(y):
            return y.reshape(bsz, t, DN_HEADS, DN_HEAD_DIM).transpose(0, 2, 1, 3)

        q = l2_normalize(heads(qkv[..., :DN_WIDTH])) * (DN_HEAD_DIM ** -0.5)
        k = l2_normalize(heads(qkv[..., DN_WIDTH:2 * DN_WIDTH]))
        v = heads(qkv[..., 2 * DN_WIDTH:]).astype(jnp.float32)
        a_dir = gate_in[..., :2 * DN_HEADS].reshape(bsz, t, 2, DN_HEADS)
        b_dir = gate_in[..., 2 * DN_HEADS:].reshape(bsz, t, 2, DN_HEADS)
        g = -jnp.exp(a_log.astype(jnp.float32)) * jax.nn.softplus(a_dir + dt_bias.astype(jnp.float32))
        beta = jax.nn.sigmoid(b_dir)
        return q, k, v, g.transpose(2, 0, 3, 1), beta.transpose(2, 0, 3, 1), z

    qc, kc, vc, gc, bc, zc = prep(h_ctx)
    ql, kl, vl, gl, bl, zl = prep(h_lat)
    s0 = jnp.zeros((h_lat.shape[0], DN_HEADS, DN_HEAD_DIM, DN_HEAD_DIM), jnp.float32)

    def flip(y):
        return jnp.flip(y, axis=2)

    oc_f, sc_f = gated_delta_chunked(qc, kc, vc, gc[0], bc[0], s0)
    ol_f, _ = gated_delta_chunked(ql, kl, vl, gl[0], bl[0], sc_f)
    oc_b, sc_b = gated_delta_chunked(flip(qc), flip(kc), flip(vc), flip(gc[1]), flip(bc[1]), s0)
    ol_b, _ = gated_delta_chunked(flip(ql), flip(kl), flip(vl), flip(gl[1]), flip(bl[1]), sc_b)

    def finish(o, z):
        bsz, t = z.shape[:2]
        o = o.transpose(0, 2, 1, 3).astype(z.dtype)
        y = rms_norm(o, out_norm) * jax.nn.silu(z.reshape(bsz, t, DN_HEADS, DN_HEAD_DIM))
        return y.reshape(bsz, t, DN_WIDTH) @ w_out

    y_lat = finish(ol_f + flip(ol_b), zl)
    y_ctx = finish(oc_f + flip(oc_b), zc) if with_ctx_out else None
    return y_lat, y_ctx


def diff_attn_mixer(h_lat, h_ctx, w_qkv, lam, subln, w_out, lambda_init, with_ctx_out):
    def prep(hs):
        bsz, t, _ = hs.shape
        proj = hs @ w_qkv
        q = proj[..., :DA_QK_WIDTH].reshape(bsz, t, DA_HEADS, 2, DA_HEAD_DIM) * (DA_HEAD_DIM ** -0.5)
        k = proj[..., DA_QK_WIDTH:2 * DA_QK_WIDTH].reshape(bsz, t, DA_HEADS, 2, DA_HEAD_DIM)
        v = proj[..., 2 * DA_QK_WIDTH:].reshape(bsz, t, DA_HEADS, DA_V_DIM)
        return q, k, v

    lamf = lam.astype(jnp.float32)
    lambda_full = jnp.exp(jnp.sum(lamf[0] * lamf[1])) - jnp.exp(jnp.sum(lamf[2] * lamf[3])) + lambda_init

    def attend(qb, keys, vals):
        s = jnp.einsum('bqhmd,bkhmd->bhmqk', qb, keys).astype(jnp.float32)
        p = jax.nn.softmax(s, axis=-1)
        p = (p[:, :, 0] - lambda_full * p[:, :, 1]).astype(vals.dtype)
        return jnp.einsum('bhqk,bkhe->bqhe', p, vals)

    def finish(o):
        bsz, t = o.shape[:2]
        y = rms_norm(o, subln) * (1.0 - lambda_init)
        return y.reshape(bsz, t, DA_V_WIDTH) @ w_out

    qc, kc, vc = prep(h_ctx)
    ql, kl, vl = prep(h_lat)
    bsz, t_lat = h_lat.shape[:2]
    cos, sin = axial_rope(t_lat, DA_HEAD_DIM)
    ql = apply_rope(ql, cos, sin)
    kl = apply_rope(kl, cos, sin)
    k_all = jnp.concatenate([kc, kl], axis=1)
    v_all = jnp.concatenate([vc, vl], axis=1)
    n_blocks = t_lat // Q_BLOCK
    q_blocks = ql.reshape(bsz, n_blocks, Q_BLOCK, DA_HEADS, 2, DA_HEAD_DIM).transpose(1, 0, 2, 3, 4, 5)
    ol = lax.map(lambda qb: attend(qb, k_all, v_all), q_blocks)
    ol = ol.transpose(1, 0, 2, 3, 4).reshape(bsz, t_lat, DA_HEADS, DA_V_DIM)
    y_lat = finish(ol)
    y_ctx = finish(attend(qc, kc, vc)) if with_ctx_out else None
    return y_lat, y_ctx


def setup_inputs(seed: int = 0) -> dict:
    key = jax.random.key(seed)
    ks = jax.random.split(key, 20)
    f32 = jnp.float32

    def normal(k, shape, scale):
        return jax.random.normal(k, shape, f32) * scale

    dt = jnp.exp(jax.random.uniform(ks[12], (N_DN_LAYERS, 2, DN_HEADS), f32,
                                    math.log(1e-3), math.log(1e-1)))
    return {
        'x': normal(ks[0], (BATCH, SEQ, D_MODEL), 1.0),
        'c': normal(ks[1], (BATCH, D_MODEL), 1.0),
        'ctx': normal(ks[2], (BATCH, CTX_LEN, D_MODEL), 1.0),
        'c_ctx': normal(ks[3], (D_MODEL,), 1.0),
        'ada_w': normal(ks[4], (DEPTH, D_MODEL, 6 * D_MODEL), 0.5 * D_MODEL ** -0.5),
        'ada_b': normal(ks[5], (DEPTH, 6 * D_MODEL), 0.02),
        'norm_w': 1.0 + normal(ks[6], (DEPTH, 2, D_MODEL), 0.02),
        'mlp_w1': normal(ks[7], (DEPTH, D_MODEL, D_FF), D_MODEL ** -0.5),
        'mlp_w2': normal(ks[8], (DEPTH, D_FF, D_MODEL), D_FF ** -0.5),
        'dn_w_in': normal(ks[9], (N_DN_LAYERS, D_MODEL, DN_PROJ), D_MODEL ** -0.5),
        'dn_conv': normal(ks[10], (N_DN_LAYERS, DN_CONV_W, 3 * DN_WIDTH), DN_CONV_W ** -0.5),
        'dn_a_log': jnp.log(jax.random.uniform(ks[11], (N_DN_LAYERS, 2, DN_HEADS), f32, 1.0, 16.0)),
        'dn_dt_bias': dt + jnp.log(-jnp.expm1(-dt)),
        'dn_out_norm': 1.0 + normal(ks[13], (N_DN_LAYERS, DN_HEAD_DIM), 0.02),
        'dn_w_out': normal(ks[14], (N_DN_LAYERS, DN_WIDTH, D_MODEL), DN_WIDTH ** -0.5),
        'da_w_qkv': normal(ks[15], (N_DA_LAYERS, D_MODEL, DA_PROJ), D_MODEL ** -0.5),
        'da_lambda': normal(ks[16], (N_DA_LAYERS, 4, DA_HEAD_DIM), 0.1),
        'da_subln': 1.0 + normal(ks[17], (N_DA_LAYERS, DA_V_DIM), 0.02),
        'da_w_out': normal(ks[18], (N_DA_LAYERS, DA_V_WIDTH, D_MODEL), DA_V_WIDTH ** -0.5),
        'final_norm': 1.0 + normal(ks[19], (D_MODEL,), 0.02),
    }


def reference(x, c, ctx, c_ctx, ada_w, ada_b, norm_w, mlp_w1, mlp_w2, dn_w_in, dn_conv, dn_a_log,
              dn_dt_bias, dn_out_norm, dn_w_out, da_w_qkv, da_lambda, da_subln, da_w_out, final_norm):
    xc = ctx
    silu_c = jax.nn.silu(c)
    silu_cc = jax.nn.silu(c_ctx)
    for i in range(DEPTH):
        last = i == DEPTH - 1
        m_lat = jnp.split((silu_c @ ada_w[i] + ada_b[i])[:, None, :], 6, axis=-1)
        m_ctx = jnp.split(silu_cc @ ada_w[i] + ada_b[i], 6, axis=-1)
        h_lat = modulate(x, norm_w[i, 0], m_lat[0], m_lat[1])
        h_ctx = modulate(xc, norm_w[i, 0], m_ctx[0], m_ctx[1])
        j = i // N_MIXERS
        if i % N_MIXERS == 0:
            y_lat, y_ctx = deltanet_mixer(h_lat, h_ctx, dn_w_in[j], dn_conv[j], dn_a_log[j], dn_dt_bias[j],
                                          dn_out_norm[j], dn_w_out[j], not last)
        else:
            lambda_init = 0.8 - 0.6 * math.exp(-0.3 * i)
            y_lat, y_ctx = diff_attn_mixer(h_lat, h_ctx, da_w_qkv[j], da_lambda[j], da_subln[j], da_w_out[j],
                                           lambda_init, not last)
        x = x + m_lat[2] * y_lat
        x = x + m_lat[5] * squared_relu_mlp(modulate(x, norm_w[i, 1], m_lat[3], m_lat[4]), mlp_w1[i], mlp_w2[i])
        if not last:
            xc = xc + m_ctx[2] * y_ctx
            xc = xc + m_ctx[5] * squared_relu_mlp(modulate(xc, norm_w[i, 1], m_ctx[3], m_ctx[4]),
                                                  mlp_w1[i], mlp_w2[i])
    return rms_norm(x, final_norm)
```

```python
import functools
import math

import jax
import jax.numpy as jnp
import numpy as np
from jax import lax
from jax.experimental import pallas as pl
from jax.experimental.pallas import tpu as pltpu

F32 = jnp.float32
BF16 = jnp.bfloat16

NORM_EPS = 1e-6
ROPE_THETA = 10000.0
GRID_W = 64
LANES = 128
SUBLANES = 8
VMEM_LIMIT = 56 * 1024 * 1024

DN_HEAD_DIM = 128
DN_CONV_W = 5
DN_CHUNK = 128
DN_BASE = 16
DN_HEADS_PER_STEP = 2
DA_HEAD_DIM = 64
DA_V_DIM = 128


def _dot(a, b):
    return jnp.dot(a, b, preferred_element_type=F32)


def _mm(a, b):
    return jnp.dot(a.astype(BF16), b.astype(BF16), preferred_element_type=F32)


def _silu(x):
    return x * jax.nn.sigmoid(x)


def _cparams(sem):
    return pltpu.CompilerParams(dimension_semantics=sem, vmem_limit_bytes=VMEM_LIMIT)


def _const_spec(shape):
    nd = len(shape)
    return pl.BlockSpec(shape, lambda *_: (0,) * nd, pipeline_mode=pl.Buffered(1))


def _mod_kernel(c_ref, w_ref, b_ref, o_ref):
    s = _silu(c_ref[...]).astype(BF16)
    o_ref[...] = _dot(s, w_ref[...].astype(BF16)) + b_ref[...]


def _modulation(cc, ada_w, ada_b):
    depth, d, n6 = ada_w.shape
    r = cc.shape[0]
    tn = n6 // 4
    return pl.pallas_call(
        _mod_kernel,
        grid=(depth, n6 // tn),
        in_specs=[
            pl.BlockSpec((r, d), lambda l, j: (0, 0)),
            pl.BlockSpec((None, d, tn), lambda l, j: (l, 0, j)),
            pl.BlockSpec((None, 1, tn), lambda l, j: (l, 0, j)),
        ],
        out_specs=pl.BlockSpec((None, r, tn), lambda l, j: (l, 0, j)),
        out_shape=jax.ShapeDtypeStruct((depth, r, n6), F32),
        compiler_params=_cparams(("arbitrary", "arbitrary")),
        name="adaln_mod",
    )(cc, ada_w, ada_b.reshape(depth, 1, n6))


def _norm_mod(x, nw, mod, shift_idx, scale_idx):
    y = x * lax.rsqrt(jnp.mean(x * x, axis=-1, keepdims=True) + NORM_EPS) * nw
    return y * (1.0 + mod[scale_idx:scale_idx + 1, :]) + mod[shift_idx:shift_idx + 1, :]


def _row_tile(rows_per_batch):
    tm = min(512, rows_per_batch)
    assert rows_per_batch % tm == 0
    return tm


def _dn_proj_kernel(x_ref, nw_ref, mod_ref, wm_ref, wg_ref, qkv_ref, z_ref, g_ref, *, width):
    h = _norm_mod(x_ref[...], nw_ref[...], mod_ref[...], 0, 1).astype(BF16)
    tn = 512
    for j in range(0, 3 * width, tn):
        qkv_ref[:, j:j + tn] = _dot(h, wm_ref[:, j:j + tn])
    for j in range(0, width, tn):
        z_ref[:, j:j + tn] = _dot(h, wm_ref[:, 3 * width + j:3 * width + j + tn])
    g_ref[...] = _dot(h, wg_ref[...])


def _dn_proj(x2d, rows_per_batch, mod_row, nw, mods_l, w_main, w_gate):
    r, d = x2d.shape
    width = w_main.shape[1] // 4
    gw = w_gate.shape[1]
    tm = _row_tile(rows_per_batch)
    per = rows_per_batch // tm
    return pl.pallas_call(
        functools.partial(_dn_proj_kernel, width=width),
        grid=(r // tm,),
        in_specs=[
            pl.BlockSpec((tm, d), lambda i: (i, 0)),
            _const_spec((1, d)),
            pl.BlockSpec((None, 6, d), lambda i: (mod_row(i // per), 0, 0)),
            _const_spec(w_main.shape),
            _const_spec(w_gate.shape),
        ],
        out_specs=[
            pl.BlockSpec((tm, 3 * width), lambda i: (i, 0)),
            pl.BlockSpec((tm, width), lambda i: (i, 0)),
            pl.BlockSpec((tm, gw), lambda i: (i, 0)),
        ],
        out_shape=[
            jax.ShapeDtypeStruct((r, 3 * width), F32),
            jax.ShapeDtypeStruct((r, width), F32),
            jax.ShapeDtypeStruct((r, gw), F32),
        ],
        compiler_params=_cparams(("arbitrary",)),
        name="dn_proj",
    )(x2d, nw, mods_l, w_main, w_gate)


def _cumsum_rows(x, reverse):
    n = x.shape[0]
    row = lax.broadcasted_iota(jnp.int32, x.shape, 0)
    s = 1
    while s < n:
        if reverse:
            x = x + jnp.where(row < n - s, pltpu.roll(x, n - s, 0), 0.0)
        else:
            x = x + jnp.where(row >= s, pltpu.roll(x, s, 0), 0.0)
        s *= 2
    return x


def _unit_tri_inverse_minus_identity(mats):
    n = mats[0].shape[0]
    row = lax.broadcasted_iota(jnp.int32, (n, n), 0)
    col = lax.broadcasted_iota(jnp.int32, (n, n), 1)

    def same_block(size):
        return (row // size) == (col // size)

    base = same_block(DN_BASE)
    ps = [jnp.where(base, -a, 0.0) for a in mats]
    accs = ps
    steps = int(math.log2(DN_BASE))
    for k in range(1, steps):
        ps = [_mm(p, p) for p in ps]
        prods = [_mm(acc, p) for acc, p in zip(accs, ps)]
        accs = [acc + p + pr for acc, p, pr in zip(accs, ps, prods)]
    size = DN_BASE
    while size < n:
        sel = same_block(2 * size) & jnp.logical_not(same_block(size))
        offs = [jnp.where(sel, a, 0.0) for a in mats]
        ys = [off + _mm(acc, off) for acc, off in zip(accs, offs)]
        accs = [acc - y - _mm(y, acc) for acc, y in zip(accs, ys)]
        size *= 2
    return accs


def _dn_core_kernel(qc_ref, kc_ref, vc_ref, ql_ref, kl_ref, vl_ref, zc_ref, zl_ref, gc_ref, gl_ref,
                    cwq_ref, cwk_ref, cwv_ref, alog_ref, dtb_ref, onorm_ref,
                    yc_ref, yl_ref,
                    xp_ref, qn_ref, kn_ref, vn_ref, knt_ref, gb_ref, gbt_ref,
                    u_ref, w_ref, qkm_ref, o_ref, st_ref, *, hb, ctx_len, lat_len):
    C = DN_CHUNK
    seq = ctx_len + lat_len
    n_ctx, n_chunks = ctx_len // C, seq // C
    pad = SUBLANES
    lat_off = ctx_len + 2 * pad
    width = hb * LANES

    lane = lax.broadcasted_iota(jnp.int32, (C, LANES), 1)
    neg_rate = -jnp.exp(alog_ref[...])
    for c in range(n_chunks):
        src = gc_ref[c * C:(c + 1) * C, :] if c < n_ctx else gl_ref[(c - n_ctx) * C:(c - n_ctx + 1) * C, :]
        a = src + dtb_ref[...]
        g = neg_rate * (jnp.maximum(a, 0.0) + jnp.log1p(jnp.exp(-jnp.abs(a))))
        pre = _cumsum_rows(g, reverse=False)
        suf = _cumsum_rows(g, reverse=True)
        vals = jnp.where(lane < hb, pre, jnp.where(lane < 2 * hb, suf, jax.nn.sigmoid(src)))
        gb_ref[c * C:(c + 1) * C, :] = vals
        gbt_ref[:, c * C:(c + 1) * C] = vals.T

    tr = 256
    zeros_pad = jnp.zeros((pad, width), F32)
    for part, (c_ref, l_ref, cw_ref, dst_ref) in enumerate((
            (qc_ref, ql_ref, cwq_ref, qn_ref), (kc_ref, kl_ref, cwk_ref, kn_ref), (vc_ref, vl_ref, cwv_ref, vn_ref))):
        xp_ref[0:pad, :] = zeros_pad
        xp_ref[pad:pad + ctx_len, :] = c_ref[...]
        xp_ref[pad + ctx_len:lat_off, :] = jnp.zeros((pad, width), F32)
        xp_ref[lat_off:lat_off + lat_len, :] = l_ref[...]
        xp_ref[lat_off + lat_len:lat_off + lat_len + pad, :] = zeros_pad
        for r0 in range(0, seq, tr):
            x0 = (pad + r0) if r0 < ctx_len else (lat_off + r0 - ctx_len)
            for j in range(hb):
                cols = slice(j * LANES, (j + 1) * LANES)
                acc = None
                for d in range(DN_CONV_W):
                    term = cw_ref[d:d + 1, cols] * xp_ref[x0 + d - 2:x0 + d - 2 + tr, cols]
                    acc = term if acc is None else acc + term
                y = _silu(acc)
                if part < 2:
                    y = y * lax.rsqrt(jnp.sum(y * y, axis=-1, keepdims=True) + 1e-6)
                if part == 0:
                    y = y * (DN_HEAD_DIM ** -0.5)
                dst_ref[r0:r0 + tr, cols] = y
                if part == 1:
                    knt_ref[cols, r0:r0 + tr] = y.T

    row = lax.broadcasted_iota(jnp.int32, (C, C), 0)
    col = lax.broadcasted_iota(jnp.int32, (C, C), 1)

    def local_body(c, carry):
        r = pl.multiple_of(c * C, C)
        gb = gb_ref[pl.ds(r, C), :]
        gbt = gbt_ref[:, pl.ds(r, C)]
        chains, a_mats, rhss = [], [], []
        for j in range(hb):
            cols = slice(j * LANES, (j + 1) * LANES)
            qn = qn_ref[pl.ds(r, C), cols]
            kn = kn_ref[pl.ds(r, C), cols]
            vn = vn_ref[pl.ds(r, C), cols]
            knt = knt_ref[cols, pl.ds(r, C)].astype(BF16)
            kk = _dot(kn.astype(BF16), knt)
            qk = _dot(qn.astype(BF16), knt)
            for dr in range(2):
                gl_, bl_ = dr * hb + j, 2 * hb + dr * hb + j
                gcol, grow, bcol = gb[:, gl_:gl_ + 1], gbt[gl_:gl_ + 1, :], gb[:, bl_:bl_ + 1]
                incl = (row >= col) if dr == 0 else (row <= col)
                strict = (row > col) if dr == 0 else (row < col)
                dec = jnp.where(incl, jnp.exp(jnp.where(incl, gcol - grow, 0.0)), 0.0)
                qkm_ref[dr, pl.ds(r, C), cols] = (qk * dec).astype(BF16)
                chains.append((dr, cols))
                a_mats.append(jnp.where(strict, kk * dec * bcol, 0.0))
                rhss.append(jnp.concatenate([vn * bcol, kn * (bcol * jnp.exp(gcol))], axis=1))
        tinvs = _unit_tri_inverse_minus_identity(a_mats)
        uws = [rhs + _mm(tinv, rhs) for tinv, rhs in zip(tinvs, rhss)]
        for (dr, cols), uw in zip(chains, uws):
            u_ref[dr, pl.ds(r, C), cols] = uw[:, :LANES]
            w_ref[dr, pl.ds(r, C), cols] = uw[:, LANES:].astype(BF16)
        return carry

    lax.fori_loop(0, n_chunks, local_body, 0)

    st_ref[...] = jnp.zeros(st_ref.shape, F32)

    def scan_body(s, carry):
        chains = []
        for dr in range(2):
            if dr == 0:
                c = s
            else:
                c = jnp.where(s < n_ctx, n_ctx - 1 - s, n_chunks - 1 - s + n_ctx)
            r = pl.multiple_of(c * C, C)
            gb = gb_ref[pl.ds(r, C), :]
            gbt = gbt_ref[:, pl.ds(r, C)]
            for j in range(hb):
                cols = slice(j * LANES, (j + 1) * LANES)
                gl_ = dr * hb + j
                gcol, grow = gb[:, gl_:gl_ + 1], gbt[gl_:gl_ + 1, :]
                g_end_c = gcol[C - 1:C, :] if dr == 0 else gcol[0:1, :]
                g_end_r = grow[:, C - 1:C] if dr == 0 else grow[:, 0:1]
                qd = (qn_ref[pl.ds(r, C), cols] * jnp.exp(gcol)).astype(BF16)
                kdt = (knt_ref[cols, pl.ds(r, C)] * jnp.exp(g_end_r - grow)).astype(BF16)
                chains.append((dr, j, r, cols, qd, kdt, jnp.exp(g_end_c)))
        states = [st_ref[dr * hb + j] for dr, j, *_ in chains]
        pss = [_dot(jnp.concatenate([w_ref[dr, pl.ds(r, C), cols], qd], axis=0), st.astype(BF16))
               for (dr, j, r, cols, qd, kdt, gt), st in zip(chains, states)]
        nus = [(u_ref[dr, pl.ds(r, C), cols] - ps[:C]).astype(BF16)
               for (dr, j, r, cols, qd, kdt, gt), ps in zip(chains, pss)]
        upds = [_dot(kdt, nu) for (dr, j, r, cols, qd, kdt, gt), nu in zip(chains, nus)]
        outs = [ps[C:] + _dot(qkm_ref[dr, pl.ds(r, C), cols], nu)
                for (dr, j, r, cols, qd, kdt, gt), ps, nu in zip(chains, pss, nus)]
        for (dr, j, r, cols, qd, kdt, gt), st, upd, out in zip(chains, states, upds, outs):
            st_ref[dr * hb + j] = st * gt + upd
            o_ref[dr, pl.ds(r, C), cols] = out
        return carry

    lax.fori_loop(0, n_chunks, scan_body, 0)

    onorm = onorm_ref[...]
    for r0 in range(0, seq, tr):
        for j in range(hb):
            cols = slice(j * LANES, (j + 1) * LANES)
            o = o_ref[0, r0:r0 + tr, cols] + o_ref[1, r0:r0 + tr, cols]
            y = o * lax.rsqrt(jnp.mean(o * o, axis=-1, keepdims=True) + NORM_EPS) * onorm
            if r0 < ctx_len:
                yc_ref[r0:r0 + tr, cols] = (y * _silu(zc_ref[r0:r0 + tr, cols])).astype(BF16)
            else:
                q0 = r0 - ctx_len
                yl_ref[q0:q0 + tr, cols] = (y * _silu(zl_ref[q0:q0 + tr, cols])).astype(BF16)


def _dn_core(qkv_c, qkv_l, z_c, z_l, g_c, g_l, conv_w, alog_g, dtb_g, onorm, batch, ctx_len, lat_len):
    hb = DN_HEADS_PER_STEP
    width = hb * LANES
    dn_width = z_c.shape[1]
    ng = dn_width // width
    seq = ctx_len + lat_len
    assert ctx_len % 256 == 0 and lat_len % 256 == 0 and ctx_len % DN_CHUNK == 0

    def col_spec(rows, off):
        return pl.BlockSpec((rows, width), lambda b, g: (b, off + g))

    in_specs = [
        col_spec(ctx_len, 0), col_spec(ctx_len, ng), col_spec(ctx_len, 2 * ng),
        col_spec(lat_len, 0), col_spec(lat_len, ng), col_spec(lat_len, 2 * ng),
        col_spec(ctx_len, 0), col_spec(lat_len, 0),
        pl.BlockSpec((ctx_len, LANES), lambda b, g: (b, g)),
        pl.BlockSpec((lat_len, LANES), lambda b, g: (b, g)),
        pl.BlockSpec((SUBLANES, width), lambda b, g: (0, g)),
        pl.BlockSpec((SUBLANES, width), lambda b, g: (0, ng + g)),
        pl.BlockSpec((SUBLANES, width), lambda b, g: (0, 2 * ng + g)),
        pl.BlockSpec((None, 1, LANES), lambda b, g: (g, 0, 0)),
        pl.BlockSpec((None, 1, LANES), lambda b, g: (g, 0, 0)),
        pl.BlockSpec((1, LANES), lambda b, g: (0, 0)),
    ]
    scratch = [
        pltpu.VMEM((seq + 3 * SUBLANES, width), F32),
        pltpu.VMEM((seq, width), F32),
        pltpu.VMEM((seq, width), F32),
        pltpu.VMEM((seq, width), F32),
        pltpu.VMEM((width, seq), F32),
        pltpu.VMEM((seq, LANES), F32),
        pltpu.VMEM((LANES, seq), F32),
        pltpu.VMEM((2, seq, width), F32),
        pltpu.VMEM((2, seq, width), BF16),
        pltpu.VMEM((2, seq, width), BF16),
        pltpu.VMEM((2, seq, width), F32),
        pltpu.VMEM((2 * hb, DN_HEAD_DIM, DN_HEAD_DIM), F32),
    ]
    return pl.pallas_call(
        functools.partial(_dn_core_kernel, hb=hb, ctx_len=ctx_len, lat_len=lat_len),
        grid=(batch, ng),
        in_specs=in_specs,
        out_specs=[col_spec(ctx_len, 0), col_spec(lat_len, 0)],
        out_shape=[jax.ShapeDtypeStruct((batch * ctx_len, dn_width), BF16),
                   jax.ShapeDtypeStruct((batch * lat_len, dn_width), BF16)],
        scratch_shapes=scratch,
        compiler_params=_cparams(("arbitrary", "arbitrary")),
        name="dn_core",
    )(qkv_c, qkv_c, qkv_c, qkv_l, qkv_l, qkv_l, z_c, z_l, g_c, g_l, conv_w, conv_w, conv_w, alog_g, dtb_g, onorm)


def _da_proj_kernel(*refs, rope, qk_width, v_width):
    if rope:
        x_ref, nw_ref, mod_ref, w_ref, cos_ref, sin_ref, q_ref, k_ref, v_ref = refs
    else:
        x_ref, nw_ref, mod_ref, w_ref, q_ref, k_ref, v_ref = refs
    h = _norm_mod(x_ref[...], nw_ref[...], mod_ref[...], 0, 1).astype(BF16)
    tm = h.shape[0]
    if rope:
        cos, sin = cos_ref[...], sin_ref[...]
        lane = lax.broadcasted_iota(jnp.int32, (tm, LANES), 1)
        first_half = (lane % DA_HEAD_DIM) < (DA_HEAD_DIM // 2)
    for which, dst in ((0, q_ref), (1, k_ref)):
        for j in range(0, qk_width, LANES):
            y = _dot(h, w_ref[:, which * qk_width + j:which * qk_width + j + LANES])
            if which == 0:
                y = y * (DA_HEAD_DIM ** -0.5)
            if rope:
                half = DA_HEAD_DIM // 2
                partner = jnp.where(first_half, pltpu.roll(y, LANES - half, 1), pltpu.roll(y, half, 1))
                y = y * cos + partner * sin
            dst[:, j:j + LANES] = y.astype(BF16)
    for j in range(0, v_width, 512):
        v_ref[:, j:j + 512] = _dot(h, w_ref[:, 2 * qk_width + j:2 * qk_width + j + 512]).astype(BF16)


def _da_proj(x2d, rows_per_batch, mod_row, nw, mods_l, w_qkv, rope_tabs, qk_width, v_width):
    r, d = x2d.shape
    tm = _row_tile(rows_per_batch)
    per = rows_per_batch // tm
    in_specs = [
        pl.BlockSpec((tm, d), lambda i: (i, 0)),
        _const_spec((1, d)),
        pl.BlockSpec((None, 6, d), lambda i: (mod_row(i // per), 0, 0)),
        _const_spec(w_qkv.shape),
    ]
    args = [x2d, nw, mods_l, w_qkv]
    if rope_tabs is not None:
        in_specs += [pl.BlockSpec((tm, LANES), lambda i: (i % per, 0))] * 2
        args += list(rope_tabs)
    return pl.pallas_call(
        functools.partial(_da_proj_kernel, rope=rope_tabs is not None, qk_width=qk_width, v_width=v_width),
        grid=(r // tm,),
        in_specs=in_specs,
        out_specs=[pl.BlockSpec((tm, qk_width), lambda i: (i, 0)),
                   pl.BlockSpec((tm, qk_width), lambda i: (i, 0)),
                   pl.BlockSpec((tm, v_width), lambda i: (i, 0))],
        out_shape=[jax.ShapeDtypeStruct((r, qk_width), BF16),
                   jax.ShapeDtypeStruct((r, qk_width), BF16),
                   jax.ShapeDtypeStruct((r, v_width), BF16)],
        compiler_params=_cparams(("arbitrary",)),
        name="da_proj",
    )(*args)


def _rope_tables(n_tokens):
    rows = n_tokens // GRID_W
    row = jnp.repeat(jnp.arange(rows, dtype=jnp.int32), GRID_W).astype(F32)
    col = jnp.tile(jnp.arange(GRID_W, dtype=jnp.int32), rows).astype(F32)
    n_freq = DA_HEAD_DIM // 4
    inv = ROPE_THETA ** (-jnp.arange(n_freq, dtype=F32) / n_freq)
    ang = jnp.concatenate([row[:, None] * inv, col[:, None] * inv], axis=-1)
    cos, sin = jnp.cos(ang), jnp.sin(ang)
    reps = LANES // DA_HEAD_DIM
    return (jnp.tile(jnp.concatenate([cos, cos], axis=-1), (1, reps)),
            jnp.tile(jnp.concatenate([-sin, sin], axis=-1), (1, reps)))


def _flash_kernel(*refs, n_seg, seg_lens, tk, lambda_init):
    q_ref = refs[0]
    k_refs = refs[1:1 + n_seg]
    v_refs = refs[1 + n_seg:1 + 2 * n_seg]
    lam_ref, subln_ref, y_ref, m_ref, l_ref, acc_ref = refs[1 + 2 * n_seg:]
    tq = q_ref.shape[0]
    lam = lam_ref[...]
    lam_full = (jnp.exp(jnp.sum(lam[0:1, :] * lam[1:2, :], axis=-1, keepdims=True))
                - jnp.exp(jnp.sum(lam[2:3, :] * lam[3:4, :], axis=-1, keepdims=True)) + lambda_init)
    q = q_ref[...]
    lane = lax.broadcasted_iota(jnp.int32, (1, LANES), 1)
    map_masks = (lane < DA_HEAD_DIM, lane >= DA_HEAD_DIM)
    m_ref[...] = jnp.full(m_ref.shape, -jnp.inf, F32)
    l_ref[...] = jnp.zeros(l_ref.shape, F32)
    acc_ref[...] = jnp.zeros(acc_ref.shape, F32)

    def tile(k, v):
        for mp in range(2):
            kz = jnp.where(map_masks[mp], k, jnp.zeros_like(k))
            s = lax.dot_general(q, kz, (((1,), (1,)), ((), ())), preferred_element_type=F32)
            m_old = m_ref[mp]
            m_new = jnp.maximum(m_old, jnp.max(s, axis=-1, keepdims=True))
            alpha = jnp.exp(m_old - m_new)
            p = jnp.exp(s - m_new)
            l_ref[mp] = alpha * l_ref[mp] + jnp.sum(p, axis=-1, keepdims=True)
            acc_ref[mp] = alpha * acc_ref[mp] + _dot(p.astype(BF16), v)
            m_ref[mp] = m_new

    for sg in range(n_seg):
        n = seg_lens[sg]
        step = min(tk, n)
        if n // step == 1:
            tile(k_refs[sg][...], v_refs[sg][...])
        else:
            def body(t, carry, sg=sg, step=step):
                r = pl.multiple_of(t * step, step)
                tile(k_refs[sg][pl.ds(r, step), :], v_refs[sg][pl.ds(r, step), :])
                return carry
            lax.fori_loop(0, n // step, body, 0)

    o = acc_ref[0] / l_ref[0] - lam_full * (acc_ref[1] / l_ref[1])
    y = o * lax.rsqrt(jnp.mean(o * o, axis=-1, keepdims=True) + NORM_EPS) * subln_ref[...]
    y_ref[...] = (y * (1.0 - lambda_init)).astype(BF16)


def _flash(q, ks, vs, seg_lens, q_rows_per_batch, lam, subln, lambda_init, batch, heads):
    tq = min(512, q_rows_per_batch)
    nq = q_rows_per_batch // tq
    n_seg = len(ks)
    in_specs = [pl.BlockSpec((tq, LANES), lambda b, h, i: (b * nq + i, h))]
    for n in list(seg_lens) * 2:
        in_specs.append(pl.BlockSpec((n, LANES), lambda b, h, i: (b, h)))
    in_specs += [pl.BlockSpec(lam.shape, lambda b, h, i: (0, 0)), pl.BlockSpec((1, LANES), lambda b, h, i: (0, 0))]
    return pl.pallas_call(
        functools.partial(_flash_kernel, n_seg=n_seg, seg_lens=tuple(seg_lens), tk=512, lambda_init=lambda_init),
        grid=(batch, heads, nq),
        in_specs=in_specs,
        out_specs=pl.BlockSpec((tq, LANES), lambda b, h, i: (b * nq + i, h)),
        out_shape=jax.ShapeDtypeStruct((q.shape[0], heads * DA_V_DIM), BF16),
        scratch_shapes=[pltpu.VMEM((2, tq, 1), F32), pltpu.VMEM((2, tq, 1), F32), pltpu.VMEM((2, tq, LANES), F32)],
        compiler_params=_cparams(("arbitrary", "arbitrary", "arbitrary")),
        name="diff_flash",
    )(q, *ks, *vs, lam, subln)


def _post_kernel(*refs, final, ff_tile):
    if final:
        x_ref, y_ref, mod_ref, wo_ref, nw_ref, w1_ref, w2_ref, fw_ref, o_ref, acc_ref = refs
    else:
        x_ref, y_ref, mod_ref, wo_ref, nw_ref, w1_ref, w2_ref, o_ref, acc_ref = refs
    mod = mod_ref[...]
    x1 = x_ref[...] + mod[2:3, :] * _dot(y_ref[...], wo_ref[...])
    h = _norm_mod(x1, nw_ref[...], mod, 3, 4).astype(BF16)
    d_ff = w1_ref.shape[1]
    for f in range(0, d_ff, ff_tile):
        a = jnp.maximum(_dot(h, w1_ref[:, f:f + ff_tile]), 0.0)
        part = _dot((a * a).astype(BF16), w2_ref[f:f + ff_tile, :])
        if f == 0:
            acc_ref[...] = part
        else:
            acc_ref[...] += part
    x2 = x1 + mod[5:6, :] * acc_ref[...]
    if final:
        x2 = x2 * lax.rsqrt(jnp.mean(x2 * x2, axis=-1, keepdims=True) + NORM_EPS) * fw_ref[...]
    o_ref[...] = x2


def _post(x2d, y2d, rows_per_batch, mod_row, mods_l, w_out, nw, w1, w2, final_w):
    r, d = x2d.shape
    tm = _row_tile(rows_per_batch)
    per = rows_per_batch // tm
    final = final_w is not None
    in_specs = [
        pl.BlockSpec((tm, d), lambda i: (i, 0)),
        pl.BlockSpec((tm, y2d.shape[1]), lambda i: (i, 0)),
        pl.BlockSpec((None, 6, d), lambda i: (mod_row(i // per), 0, 0)),
        _const_spec(w_out.shape),
        _const_spec((1, d)),
        _const_spec(w1.shape),
        _const_spec(w2.shape),
    ]
    args = [x2d, y2d, mods_l, w_out, nw, w1, w2]
    if final:
        in_specs.append(_const_spec((1, d)))
        args.append(final_w)
    return pl.pallas_call(
        functools.partial(_post_kernel, final=final, ff_tile=1024),
        grid=(r // tm,),
        in_specs=in_specs,
        out_specs=pl.BlockSpec((tm, d), lambda i: (i, 0)),
        out_shape=jax.ShapeDtypeStruct((r, d), F32),
        scratch_shapes=[pltpu.VMEM((tm, d), F32)],
        input_output_aliases={0: 0},
        compiler_params=_cparams(("arbitrary",)),
        name="post_mlp",
    )(*args)


def _dn_gate_layout(n_heads, hb):
    ng = n_heads // hb
    idx = np.full((ng, LANES), -1, np.int64)
    for g in range(ng):
        for kind in range(4):
            for j in range(hb):
                idx[g, kind * hb + j] = kind * n_heads + g * hb + j
    return idx.reshape(-1)


def kernel(x, c, ctx, c_ctx, ada_w, ada_b, norm_w, mlp_w1, mlp_w2, dn_w_in, dn_conv, dn_a_log, dn_dt_bias,
           dn_out_norm, dn_w_out, da_w_qkv, da_lambda, da_subln, da_w_out, final_norm):
    batch, lat_len, d = x.shape
    ctx_len = ctx.shape[1]
    depth = ada_w.shape[0]
    dn_width = dn_w_out.shape[1]
    dn_heads = dn_width // DN_HEAD_DIM
    da_v_width = da_w_out.shape[1]
    da_heads = da_v_width // DA_V_DIM
    da_qk_width = da_heads * 2 * DA_HEAD_DIM
    hb = DN_HEADS_PER_STEP

    n_rows = -(-(batch + 1) // SUBLANES) * SUBLANES
    cc = jnp.zeros((n_rows, d), F32).at[:batch].set(c).at[batch].set(c_ctx)
    mods = _modulation(cc, ada_w, ada_b).reshape(depth, n_rows, 6, d)
    lat_row = lambda b: b
    ctx_row = lambda b: batch

    xl = x.reshape(batch * lat_len, d)
    xc = ctx.reshape(batch * ctx_len, d)

    gate_idx = _dn_gate_layout(dn_heads, hb)
    gate_valid = jnp.asarray(gate_idx >= 0)
    gate_src = jnp.asarray(np.maximum(gate_idx, 0))
    ng = dn_heads // hb
    rope_tabs = _rope_tables(lat_len)

    for i in range(depth):
        last = i == depth - 1
        mods_l = mods[i]
        j = i // 2
        if i % 2 == 0:
            w_in = dn_w_in[j]
            w_main = w_in[:, :4 * dn_width].astype(BF16)
            w_gate = jnp.where(gate_valid[None, :], jnp.take(w_in[:, 4 * dn_width:], gate_src, axis=1), 0.0).astype(BF16)
            conv_w = jnp.zeros((SUBLANES, 3 * dn_width), F32).at[:DN_CONV_W].set(dn_conv[j])
            alog_g = jnp.zeros((ng, 1, LANES), F32).at[:, 0, :2 * hb].set(
                jnp.transpose(dn_a_log[j].reshape(2, ng, hb), (1, 0, 2)).reshape(ng, 2 * hb))
            dtb_g = jnp.zeros((ng, 1, LANES), F32).at[:, 0, :2 * hb].set(
                jnp.transpose(dn_dt_bias[j].reshape(2, ng, hb), (1, 0, 2)).reshape(ng, 2 * hb))
            nw0 = norm_w[i, 0].reshape(1, d)
            qkv_l, z_l, g_l = _dn_proj(xl, lat_len, lat_row, nw0, mods_l, w_main, w_gate)
            qkv_c, z_c, g_c = _dn_proj(xc, ctx_len, ctx_row, nw0, mods_l, w_main, w_gate)
            y_c, y_l = _dn_core(qkv_c, qkv_l, z_c, z_l, g_c, g_l, conv_w, alog_g, dtb_g,
                                dn_out_norm[j].reshape(1, DN_HEAD_DIM), batch, ctx_len, lat_len)
            w_out = dn_w_out[j].astype(BF16)
        else:
            lambda_init = 0.8 - 0.6 * math.exp(-0.3 * i)
            w_qkv = da_w_qkv[j].astype(BF16)
            nw0 = norm_w[i, 0].reshape(1, d)
            q_l, k_l, v_l = _da_proj(xl, lat_len, lat_row, nw0, mods_l, w_qkv, rope_tabs, da_qk_width, da_v_width)
            q_c, k_c, v_c = _da_proj(xc, ctx_len, ctx_row, nw0, mods_l, w_qkv, None, da_qk_width, da_v_width)
            lam = da_lambda[j]
            subln = da_subln[j].reshape(1, DA_V_DIM)
            y_l = _flash(q_l, (k_c, k_l), (v_c, v_l), (ctx_len, lat_len), lat_len, lam, subln, lambda_init,
                         batch, da_heads)
            y_c = None if last else _flash(q_c, (k_c,), (v_c,), (ctx_len,), ctx_len, lam, subln, lambda_init,
                                           batch, da_heads)
            w_out = da_w_out[j].astype(BF16)
        nw1 = norm_w[i, 1].reshape(1, d)
        w1 = mlp_w1[i].astype(BF16)
        w2 = mlp_w2[i].astype(BF16)
        xl = _post(xl, y_l, lat_len, lat_row, mods_l, w_out, nw1, w1, w2,
                   final_norm.reshape(1, d) if last else None)
        if not last:
            xc = _post(xc, y_c, ctx_len, ctx_row, mods_l, w_out, nw1, w1, w2, None)
    return xl.reshape(batch, lat_len, d)
```

```python
import functools
import math

import jax
import jax.numpy as jnp
import numpy as np
from jax import lax
from jax.experimental import pallas as pl
from jax.experimental.pallas import tpu as pltpu

F32 = jnp.float32
BF16 = jnp.bfloat16

NORM_EPS = 1e-6
ROPE_THETA = 10000.0
GRID_W = 64
LANES = 128
SUBLANES = 8
VMEM_LIMIT = 56 * 1024 * 1024

DN_HEAD_DIM = 128
DN_CONV_W = 5
DN_CHUNK = 128
DN_BASE = 16
DN_HEADS_PER_STEP = 2
DA_HEAD_DIM = 64
DA_V_DIM = 128
LOG2_E = 1.4426950408889634


def _dot(a, b):
    return jnp.dot(a, b, preferred_element_type=F32)


def _mm(a, b):
    return jnp.dot(a.astype(BF16), b.astype(BF16), preferred_element_type=F32)


def _silu(x):
    return x * jax.nn.sigmoid(x)


def _cparams(sem):
    return pltpu.CompilerParams(dimension_semantics=sem, vmem_limit_bytes=VMEM_LIMIT)


def _const_spec(shape):
    nd = len(shape)
    return pl.BlockSpec(shape, lambda *_: (0,) * nd, pipeline_mode=pl.Buffered(1))


def _mod_kernel(c_ref, w_ref, b_ref, o_ref):
    s = _silu(c_ref[...]).astype(BF16)
    o_ref[...] = _dot(s, w_ref[...].astype(BF16)) + b_ref[...]


def _modulation(cc, ada_w, ada_b):
    depth, d, n6 = ada_w.shape
    r = cc.shape[0]
    tn = n6 // 4
    return pl.pallas_call(
        _mod_kernel,
        grid=(depth, n6 // tn),
        in_specs=[
            pl.BlockSpec((r, d), lambda l, j: (0, 0)),
            pl.BlockSpec((None, d, tn), lambda l, j: (l, 0, j)),
            pl.BlockSpec((None, 1, tn), lambda l, j: (l, 0, j)),
        ],
        out_specs=pl.BlockSpec((None, r, tn), lambda l, j: (l, 0, j)),
        out_shape=jax.ShapeDtypeStruct((depth, r, n6), F32),
        compiler_params=_cparams(("arbitrary", "arbitrary")),
        name="adaln_mod",
    )(cc, ada_w, ada_b.reshape(depth, 1, n6))


def _norm_mod(x, nw, mod, shift_idx, scale_idx):
    y = x * lax.rsqrt(jnp.mean(x * x, axis=-1, keepdims=True) + NORM_EPS) * nw
    return y * (1.0 + mod[scale_idx:scale_idx + 1, :]) + mod[shift_idx:shift_idx + 1, :]


def _row_tile(rows_per_batch):
    tm = min(512, rows_per_batch)
    assert rows_per_batch % tm == 0
    return tm


def _dn_proj_kernel(x_ref, nw_ref, mod_ref, wm_ref, wg_ref, qkv_ref, z_ref, g_ref, *, width):
    h = _norm_mod(x_ref[...], nw_ref[...], mod_ref[...], 0, 1).astype(BF16)
    tn = 512
    for j in range(0, 3 * width, tn):
        qkv_ref[:, j:j + tn] = _dot(h, wm_ref[:, j:j + tn])
    for j in range(0, width, tn):
        z_ref[:, j:j + tn] = _dot(h, wm_ref[:, 3 * width + j:3 * width + j + tn])
    g_ref[...] = _dot(h, wg_ref[...])


def _dn_proj(x2d, rows_per_batch, mod_row, nw, mods_l, w_main, w_gate):
    r, d = x2d.shape
    width = w_main.shape[1] // 4
    gw = w_gate.shape[1]
    tm = _row_tile(rows_per_batch)
    per = rows_per_batch // tm
    return pl.pallas_call(
        functools.partial(_dn_proj_kernel, width=width),
        grid=(r // tm,),
        in_specs=[
            pl.BlockSpec((tm, d), lambda i: (i, 0)),
            _const_spec((1, d)),
            pl.BlockSpec((None, 6, d), lambda i: (mod_row(i // per), 0, 0)),
            _const_spec(w_main.shape),
            _const_spec(w_gate.shape),
        ],
        out_specs=[
            pl.BlockSpec((tm, 3 * width), lambda i: (i, 0)),
            pl.BlockSpec((tm, width), lambda i: (i, 0)),
            pl.BlockSpec((tm, gw), lambda i: (i, 0)),
        ],
        out_shape=[
            jax.ShapeDtypeStruct((r, 3 * width), F32),
            jax.ShapeDtypeStruct((r, width), F32),
            jax.ShapeDtypeStruct((r, gw), F32),
        ],
        compiler_params=_cparams(("arbitrary",)),
        name="dn_proj",
    )(x2d, nw, mods_l, w_main, w_gate)


def _cumsum_rows(x, reverse):
    n = x.shape[0]
    row = lax.broadcasted_iota(jnp.int32, x.shape, 0)
    s = 1
    while s < n:
        if reverse:
            x = x + jnp.where(row < n - s, pltpu.roll(x, n - s, 0), 0.0)
        else:
            x = x + jnp.where(row >= s, pltpu.roll(x, s, 0), 0.0)
        s *= 2
    return x


def _unit_tri_inverse_minus_identity(mats):
    n = mats[0].shape[0]
    row = lax.broadcasted_iota(jnp.int32, (n, n), 0)
    col = lax.broadcasted_iota(jnp.int32, (n, n), 1)

    def same_block(size):
        return (row // size) == (col // size)

    base = same_block(DN_BASE)
    ps = [jnp.where(base, -a, 0.0) for a in mats]
    accs = ps
    steps = int(math.log2(DN_BASE))
    for k in range(1, steps):
        ps = [_mm(p, p) for p in ps]
        prods = [_mm(acc, p) for acc, p in zip(accs, ps)]
        accs = [acc + p + pr for acc, p, pr in zip(accs, ps, prods)]
    size = DN_BASE
    while size < n:
        sel = same_block(2 * size) & jnp.logical_not(same_block(size))
        offs = [jnp.where(sel, a, 0.0) for a in mats]
        ys = [off + _mm(acc, off) for acc, off in zip(accs, offs)]
        accs = [acc - y - _mm(y, acc) for acc, y in zip(accs, ys)]
        size *= 2
    return accs


def _dn_core_kernel(qc_ref, kc_ref, vc_ref, ql_ref, kl_ref, vl_ref, zc_ref, zl_ref, gc_ref, gl_ref,
                    cwq_ref, cwk_ref, cwv_ref, alog_ref, dtb_ref, onorm_ref,
                    yc_ref, yl_ref,
                    xp_ref, qn_ref, kn_ref, vn_ref, knt_ref, gb_ref, gbt_ref,
                    u_ref, w_ref, qkm_ref, st_ref, *, hb, ctx_len, lat_len):
    C = DN_CHUNK
    seq = ctx_len + lat_len
    n_ctx, n_chunks = ctx_len // C, seq // C
    pad = SUBLANES
    lat_off = ctx_len + 2 * pad
    width = hb * LANES

    lane = lax.broadcasted_iota(jnp.int32, (C, LANES), 1)
    neg_rate = -jnp.exp(alog_ref[...])
    for c in range(n_chunks):
        src = gc_ref[c * C:(c + 1) * C, :] if c < n_ctx else gl_ref[(c - n_ctx) * C:(c - n_ctx + 1) * C, :]
        a = src + dtb_ref[...]
        g = neg_rate * (jnp.maximum(a, 0.0) + jnp.log1p(jnp.exp(-jnp.abs(a))))
        pre = _cumsum_rows(g, reverse=False)
        suf = _cumsum_rows(g, reverse=True)
        vals = jnp.where(lane < hb, pre, jnp.where(lane < 2 * hb, suf, jax.nn.sigmoid(src)))
        gb_ref[c * C:(c + 1) * C, :] = vals
        gbt_ref[:, c * C:(c + 1) * C] = vals.T

    tr = 256
    zeros_pad = jnp.zeros((pad, width), F32)
    for part, (c_ref, l_ref, cw_ref, dst_ref) in enumerate((
            (qc_ref, ql_ref, cwq_ref, qn_ref), (kc_ref, kl_ref, cwk_ref, kn_ref), (vc_ref, vl_ref, cwv_ref, vn_ref))):
        xp_ref[0:pad, :] = zeros_pad
        xp_ref[pad:pad + ctx_len, :] = c_ref[...]
        xp_ref[pad + ctx_len:lat_off, :] = jnp.zeros((pad, width), F32)
        xp_ref[lat_off:lat_off + lat_len, :] = l_ref[...]
        xp_ref[lat_off + lat_len:lat_off + lat_len + pad, :] = zeros_pad
        for r0 in range(0, seq, tr):
            x0 = (pad + r0) if r0 < ctx_len else (lat_off + r0 - ctx_len)
            for j in range(hb):
                cols = slice(j * LANES, (j + 1) * LANES)
                acc = None
                for d in range(DN_CONV_W):
                    term = cw_ref[d:d + 1, cols] * xp_ref[x0 + d - 2:x0 + d - 2 + tr, cols]
                    acc = term if acc is None else acc + term
                y = _silu(acc)
                if part < 2:
                    y = y * lax.rsqrt(jnp.sum(y * y, axis=-1, keepdims=True) + 1e-6)
                if part == 0:
                    y = y * (DN_HEAD_DIM ** -0.5)
                dst_ref[r0:r0 + tr, cols] = y
                if part == 1:
                    knt_ref[cols, r0:r0 + tr] = y.T

    row = lax.broadcasted_iota(jnp.int32, (C, C), 0)
    col = lax.broadcasted_iota(jnp.int32, (C, C), 1)

    def local_body(it, carry):
        chains, a_mats, rhss = [], [], []
        for sub in range(per_iter):
            r = pl.multiple_of((it * per_iter + sub) * C, C)
            gb = gb_ref[pl.ds(r, C), :]
            gbt = gbt_ref[:, pl.ds(r, C)]
            for j in range(hb):
                cols = slice(j * LANES, (j + 1) * LANES)
                qn = qn_ref[pl.ds(r, C), cols]
                kn = kn_ref[pl.ds(r, C), cols]
                vn = vn_ref[pl.ds(r, C), cols]
                knt = knt_ref[cols, pl.ds(r, C)].astype(BF16)
                kk = _dot(kn.astype(BF16), knt)
                qk = _dot(qn.astype(BF16), knt)
                for dr in range(2):
                    gl_, bl_ = dr * hb + j, 2 * hb + dr * hb + j
                    gcol, grow, bcol = gb[:, gl_:gl_ + 1], gbt[gl_:gl_ + 1, :], gb[:, bl_:bl_ + 1]
                    incl = (row >= col) if dr == 0 else (row <= col)
                    strict = (row > col) if dr == 0 else (row < col)
                    dec = jnp.where(incl, jnp.exp(jnp.where(incl, gcol - grow, 0.0)), 0.0)
                    qkm_ref[dr, pl.ds(r, C), cols] = (qk * dec).astype(BF16)
                    chains.append((dr, r, cols))
                    a_mats.append(jnp.where(strict, kk * dec * bcol, 0.0))
                    rhss.append(jnp.concatenate([vn * bcol, kn * (bcol * jnp.exp(gcol))], axis=1))
        tinvs = _unit_tri_inverse_minus_identity(a_mats)
        uws = [rhs + _mm(tinv, rhs) for tinv, rhs in zip(tinvs, rhss)]
        for (dr, r, cols), uw in zip(chains, uws):
            u_ref[dr, pl.ds(r, C), cols] = uw[:, :LANES]
            w_ref[dr, pl.ds(r, C), cols] = uw[:, LANES:].astype(BF16)
        return carry

    per_iter = 3 if n_chunks % 3 == 0 else 2
    assert n_chunks % per_iter == 0
    lax.fori_loop(0, n_chunks // per_iter, local_body, 0)

    st_ref[...] = jnp.zeros(st_ref.shape, F32)
    xp_ref[0:seq, :] = jnp.zeros((seq, width), F32)

    def scan_body(s, carry):
        chains = []
        for dr in range(2):
            if dr == 0:
                c = s
            else:
                c = jnp.where(s < n_ctx, n_ctx - 1 - s, n_chunks - 1 - s + n_ctx)
            r = pl.multiple_of(c * C, C)
            gb = gb_ref[pl.ds(r, C), :]
            gbt = gbt_ref[:, pl.ds(r, C)]
            for j in range(hb):
                cols = slice(j * LANES, (j + 1) * LANES)
                gl_ = dr * hb + j
                gcol, grow = gb[:, gl_:gl_ + 1], gbt[gl_:gl_ + 1, :]
                g_end_c = gcol[C - 1:C, :] if dr == 0 else gcol[0:1, :]
                g_end_r = grow[:, C - 1:C] if dr == 0 else grow[:, 0:1]
                qd = (qn_ref[pl.ds(r, C), cols] * jnp.exp(gcol)).astype(BF16)
                kdt = (knt_ref[cols, pl.ds(r, C)] * jnp.exp(g_end_r - grow)).astype(BF16)
                chains.append((dr, j, r, cols, qd, kdt, jnp.exp(g_end_c)))
        states = [st_ref[dr * hb + j] for dr, j, *_ in chains]
        pss = [_dot(jnp.concatenate([w_ref[dr, pl.ds(r, C), cols], qd], axis=0), st.astype(BF16))
               for (dr, j, r, cols, qd, kdt, gt), st in zip(chains, states)]
        nus = [(u_ref[dr, pl.ds(r, C), cols] - ps[:C]).astype(BF16)
               for (dr, j, r, cols, qd, kdt, gt), ps in zip(chains, pss)]
        upds = [_dot(kdt, nu) for (dr, j, r, cols, qd, kdt, gt), nu in zip(chains, nus)]
        outs = [ps[C:] + _dot(qkm_ref[dr, pl.ds(r, C), cols], nu)
                for (dr, j, r, cols, qd, kdt, gt), ps, nu in zip(chains, pss, nus)]
        for (dr, j, r, cols, qd, kdt, gt), st, upd, out in zip(chains, states, upds, outs):
            st_ref[dr * hb + j] = st * gt + upd
            xp_ref[pl.ds(r, C), cols] += out
        return carry

    lax.fori_loop(0, n_chunks, scan_body, 0)

    onorm = onorm_ref[...]
    for r0 in range(0, seq, tr):
        for j in range(hb):
            cols = slice(j * LANES, (j + 1) * LANES)
            o = xp_ref[r0:r0 + tr, cols]
            y = o * lax.rsqrt(jnp.mean(o * o, axis=-1, keepdims=True) + NORM_EPS) * onorm
            if r0 < ctx_len:
                yc_ref[r0:r0 + tr, cols] = (y * _silu(zc_ref[r0:r0 + tr, cols])).astype(BF16)
            else:
                q0 = r0 - ctx_len
                yl_ref[q0:q0 + tr, cols] = (y * _silu(zl_ref[q0:q0 + tr, cols])).astype(BF16)


def _dn_core(qkv_c, qkv_l, z_c, z_l, g_c, g_l, conv_w, alog_g, dtb_g, onorm, batch, ctx_len, lat_len):
    hb = DN_HEADS_PER_STEP
    width = hb * LANES
    dn_width = z_c.shape[1]
    ng = dn_width // width
    seq = ctx_len + lat_len
    assert ctx_len % 256 == 0 and lat_len % 256 == 0 and ctx_len % DN_CHUNK == 0

    def col_spec(rows, off):
        return pl.BlockSpec((rows, width), lambda b, g: (b, off + g))

    in_specs = [
        col_spec(ctx_len, 0), col_spec(ctx_len, ng), col_spec(ctx_len, 2 * ng),
        col_spec(lat_len, 0), col_spec(lat_len, ng), col_spec(lat_len, 2 * ng),
        col_spec(ctx_len, 0), col_spec(lat_len, 0),
        pl.BlockSpec((ctx_len, LANES), lambda b, g: (b, g)),
        pl.BlockSpec((lat_len, LANES), lambda b, g: (b, g)),
        pl.BlockSpec((SUBLANES, width), lambda b, g: (0, g)),
        pl.BlockSpec((SUBLANES, width), lambda b, g: (0, ng + g)),
        pl.BlockSpec((SUBLANES, width), lambda b, g: (0, 2 * ng + g)),
        pl.BlockSpec((None, 1, LANES), lambda b, g: (g, 0, 0)),
        pl.BlockSpec((None, 1, LANES), lambda b, g: (g, 0, 0)),
        pl.BlockSpec((1, LANES), lambda b, g: (0, 0)),
    ]
    scratch = [
        pltpu.VMEM((seq + 3 * SUBLANES, width), F32),
        pltpu.VMEM((seq, width), F32),
        pltpu.VMEM((seq, width), F32),
        pltpu.VMEM((seq, width), F32),
        pltpu.VMEM((width, seq), F32),
        pltpu.VMEM((seq, LANES), F32),
        pltpu.VMEM((LANES, seq), F32),
        pltpu.VMEM((2, seq, width), F32),
        pltpu.VMEM((2, seq, width), BF16),
        pltpu.VMEM((2, seq, width), BF16),
        pltpu.VMEM((2 * hb, DN_HEAD_DIM, DN_HEAD_DIM), F32),
    ]
    return pl.pallas_call(
        functools.partial(_dn_core_kernel, hb=hb, ctx_len=ctx_len, lat_len=lat_len),
        grid=(batch, ng),
        in_specs=in_specs,
        out_specs=[col_spec(ctx_len, 0), col_spec(lat_len, 0)],
        out_shape=[jax.ShapeDtypeStruct((batch * ctx_len, dn_width), BF16),
                   jax.ShapeDtypeStruct((batch * lat_len, dn_width), BF16)],
        scratch_shapes=scratch,
        compiler_params=_cparams(("arbitrary", "arbitrary")),
        name="dn_core",
    )(qkv_c, qkv_c, qkv_c, qkv_l, qkv_l, qkv_l, z_c, z_l, g_c, g_l, conv_w, conv_w, conv_w, alog_g, dtb_g, onorm)


def _da_proj_kernel(*refs, rope, qk_width, v_width):
    if rope:
        x_ref, nw_ref, mod_ref, w_ref, cos_ref, sin_ref, q_ref, k_ref, v_ref = refs
    else:
        x_ref, nw_ref, mod_ref, w_ref, q_ref, k_ref, v_ref = refs
    h = _norm_mod(x_ref[...], nw_ref[...], mod_ref[...], 0, 1).astype(BF16)
    tm = h.shape[0]
    if rope:
        cos, sin = cos_ref[...], sin_ref[...]
        lane = lax.broadcasted_iota(jnp.int32, (tm, LANES), 1)
        first_half = (lane % DA_HEAD_DIM) < (DA_HEAD_DIM // 2)
    for which, dst in ((0, q_ref), (1, k_ref)):
        for j in range(0, qk_width, LANES):
            y = _dot(h, w_ref[:, which * qk_width + j:which * qk_width + j + LANES])
            if which == 0:
                y = y * (DA_HEAD_DIM ** -0.5 * LOG2_E)
            if rope:
                half = DA_HEAD_DIM // 2
                partner = jnp.where(first_half, pltpu.roll(y, LANES - half, 1), pltpu.roll(y, half, 1))
                y = y * cos + partner * sin
            dst[:, j:j + LANES] = y.astype(BF16)
    for j in range(0, v_width, 512):
        v_ref[:, j:j + 512] = _dot(h, w_ref[:, 2 * qk_width + j:2 * qk_width + j + 512]).astype(BF16)


def _da_proj(x2d, rows_per_batch, mod_row, nw, mods_l, w_qkv, rope_tabs, qk_width, v_width):
    r, d = x2d.shape
    tm = _row_tile(rows_per_batch)
    per = rows_per_batch // tm
    in_specs = [
        pl.BlockSpec((tm, d), lambda i: (i, 0)),
        _const_spec((1, d)),
        pl.BlockSpec((None, 6, d), lambda i: (mod_row(i // per), 0, 0)),
        _const_spec(w_qkv.shape),
    ]
    args = [x2d, nw, mods_l, w_qkv]
    if rope_tabs is not None:
        in_specs += [pl.BlockSpec((tm, LANES), lambda i: (i % per, 0))] * 2
        args += list(rope_tabs)
    return pl.pallas_call(
        functools.partial(_da_proj_kernel, rope=rope_tabs is not None, qk_width=qk_width, v_width=v_width),
        grid=(r // tm,),
        in_specs=in_specs,
        out_specs=[pl.BlockSpec((tm, qk_width), lambda i: (i, 0)),
                   pl.BlockSpec((tm, qk_width), lambda i: (i, 0)),
                   pl.BlockSpec((tm, v_width), lambda i: (i, 0))],
        out_shape=[jax.ShapeDtypeStruct((r, qk_width), BF16),
                   jax.ShapeDtypeStruct((r, qk_width), BF16),
                   jax.ShapeDtypeStruct((r, v_width), BF16)],
        compiler_params=_cparams(("arbitrary",)),
        name="da_proj",
    )(*args)


def _rope_tables(n_tokens):
    rows = n_tokens // GRID_W
    row = jnp.repeat(jnp.arange(rows, dtype=jnp.int32), GRID_W).astype(F32)
    col = jnp.tile(jnp.arange(GRID_W, dtype=jnp.int32), rows).astype(F32)
    n_freq = DA_HEAD_DIM // 4
    inv = ROPE_THETA ** (-jnp.arange(n_freq, dtype=F32) / n_freq)
    ang = jnp.concatenate([row[:, None] * inv, col[:, None] * inv], axis=-1)
    cos, sin = jnp.cos(ang), jnp.sin(ang)
    reps = LANES // DA_HEAD_DIM
    return (jnp.tile(jnp.concatenate([cos, cos], axis=-1), (1, reps)),
            jnp.tile(jnp.concatenate([-sin, sin], axis=-1), (1, reps)))


def _flash_kernel(*refs, n_seg, seg_lens, tk, sub_rows, lambda_init):
    q_ref = refs[0]
    k_refs = refs[1:1 + n_seg]
    v_refs = refs[1 + n_seg:1 + 2 * n_seg]
    lam_ref, subln_ref, y_ref, s_ref, mx_ref, sum_ref = refs[1 + 2 * n_seg:]
    tq = q_ref.shape[0]
    lam = lam_ref[...]
    lam_full = (jnp.exp(jnp.sum(lam[0:1, :] * lam[1:2, :], axis=-1, keepdims=True))
                - jnp.exp(jnp.sum(lam[2:3, :] * lam[3:4, :], axis=-1, keepdims=True)) + lambda_init)
    lane = lax.broadcasted_iota(jnp.int32, (1, LANES), 1)
    map_masks = (lane < DA_HEAD_DIM, lane >= DA_HEAD_DIM)
    tiles = [(sg, r) for sg in range(n_seg) for r in range(0, seg_lens[sg], tk)]
    halves = [slice(h * LANES, (h + 1) * LANES) for h in range(tk // LANES)]
    sub = min(sub_rows, tq)

    def pass_a(rows):
        q = q_ref[rows, :]
        for mp in range(2):
            qz = jnp.where(map_masks[mp], q, jnp.zeros_like(q))
            run_max = None
            for t, (sg, r) in enumerate(tiles):
                k = k_refs[sg][r:r + tk, :]
                s = lax.dot_general(qz, k, (((1,), (1,)), ((), ())), preferred_element_type=F32)
                s_ref[mp, t, rows, :] = s
                for hs in halves:
                    run_max = s[:, hs] if run_max is None else jnp.maximum(run_max, s[:, hs])
            mx_ref[mp, rows, :] = jnp.broadcast_to(jnp.max(run_max, axis=-1, keepdims=True), (sub, LANES))

    def pass_b(rows):
        for mp in range(2):
            row_max = jnp.concatenate([mx_ref[mp, rows, :]] * len(halves), axis=1)
            run_sum = None
            for t in range(len(tiles)):
                e = jnp.exp2(s_ref[mp, t, rows, :] - row_max)
                s_ref[mp, t, rows, :] = e
                for hs in halves:
                    run_sum = e[:, hs] if run_sum is None else run_sum + e[:, hs]
            total = jnp.sum(run_sum, axis=-1, keepdims=True)
            coef = (1.0 / total) if mp == 0 else (lam_full / total)
            sum_ref[mp, rows, :] = jnp.broadcast_to(coef, (sub, LANES))

    def pass_c(rows):
        c0 = jnp.concatenate([sum_ref[0, rows, :]] * len(halves), axis=1)
        c1 = jnp.concatenate([sum_ref[1, rows, :]] * len(halves), axis=1)
        acc = None
        for t, (sg, r) in enumerate(tiles):
            p = (s_ref[0, t, rows, :] * c0 - s_ref[1, t, rows, :] * c1).astype(BF16)
            part = _dot(p, v_refs[sg][r:r + tk, :])
            acc = part if acc is None else acc + part
        y = acc * lax.rsqrt(jnp.mean(acc * acc, axis=-1, keepdims=True) + NORM_EPS) * subln_ref[...]
        y_ref[rows, :] = (y * (1.0 - lambda_init)).astype(BF16)

    subs = [slice(r0, r0 + sub) for r0 in range(0, tq, sub)]
    pass_a(subs[0])
    for u in range(1, len(subs)):
        pass_a(subs[u])
        pass_b(subs[u - 1])
        pass_c(subs[u - 1])
    pass_b(subs[-1])
    pass_c(subs[-1])


def _flash(q, ks, vs, seg_lens, q_rows_per_batch, lam, subln, lambda_init, batch, heads):
    tq = min(512, q_rows_per_batch)
    tk = 256
    nq = q_rows_per_batch // tq
    n_seg = len(ks)
    n_tiles = sum(n // tk for n in seg_lens)
    assert all(n % tk == 0 for n in seg_lens)
    in_specs = [pl.BlockSpec((tq, LANES), lambda b, h, i: (b * nq + i, h))]
    for n in list(seg_lens) * 2:
        in_specs.append(pl.BlockSpec((n, LANES), lambda b, h, i: (b, h)))
    in_specs += [pl.BlockSpec(lam.shape, lambda b, h, i: (0, 0)), pl.BlockSpec((1, LANES), lambda b, h, i: (0, 0))]
    return pl.pallas_call(
        functools.partial(_flash_kernel, n_seg=n_seg, seg_lens=tuple(seg_lens), tk=tk, sub_rows=256,
                          lambda_init=lambda_init),
        grid=(batch, heads, nq),
        in_specs=in_specs,
        out_specs=pl.BlockSpec((tq, LANES), lambda b, h, i: (b * nq + i, h)),
        out_shape=jax.ShapeDtypeStruct((q.shape[0], heads * DA_V_DIM), BF16),
        scratch_shapes=[pltpu.VMEM((2, n_tiles, tq, tk), F32),
                        pltpu.VMEM((2, tq, LANES), F32),
                        pltpu.VMEM((2, tq, LANES), F32)],
        compiler_params=_cparams(("arbitrary", "arbitrary", "arbitrary")),
        name="diff_flash",
    )(q, *ks, *vs, lam, subln)


def _post_kernel(*refs, final, ff_tile):
    if final:
        x_ref, y_ref, mod_ref, wo_ref, nw_ref, w1_ref, w2_ref, fw_ref, o_ref, acc_ref = refs
    else:
        x_ref, y_ref, mod_ref, wo_ref, nw_ref, w1_ref, w2_ref, o_ref, acc_ref = refs
    mod = mod_ref[...]
    x1 = x_ref[...] + mod[2:3, :] * _dot(y_ref[...], wo_ref[...])
    h = _norm_mod(x1, nw_ref[...], mod, 3, 4).astype(BF16)
    d_ff = w1_ref.shape[1]
    for f in range(0, d_ff, ff_tile):
        a = jnp.maximum(_dot(h, w1_ref[:, f:f + ff_tile]), 0.0)
        part = _dot((a * a).astype(BF16), w2_ref[f:f + ff_tile, :])
        if f == 0:
            acc_ref[...] = part
        else:
            acc_ref[...] += part
    x2 = x1 + mod[5:6, :] * acc_ref[...]
    if final:
        x2 = x2 * lax.rsqrt(jnp.mean(x2 * x2, axis=-1, keepdims=True) + NORM_EPS) * fw_ref[...]
    o_ref[...] = x2


def _post(x2d, y2d, rows_per_batch, mod_row, mods_l, w_out, nw, w1, w2, final_w):
    r, d = x2d.shape
    tm = _row_tile(rows_per_batch)
    per = rows_per_batch // tm
    final = final_w is not None
    in_specs = [
        pl.BlockSpec((tm, d), lambda i: (i, 0)),
        pl.BlockSpec((tm, y2d.shape[1]), lambda i: (i, 0)),
        pl.BlockSpec((None, 6, d), lambda i: (mod_row(i // per), 0, 0)),
        _const_spec(w_out.shape),
        _const_spec((1, d)),
        _const_spec(w1.shape),
        _const_spec(w2.shape),
    ]
    args = [x2d, y2d, mods_l, w_out, nw, w1, w2]
    if final:
        in_specs.append(_const_spec((1, d)))
        args.append(final_w)
    return pl.pallas_call(
        functools.partial(_post_kernel, final=final, ff_tile=1024),
        grid=(r // tm,),
        in_specs=in_specs,
        out_specs=pl.BlockSpec((tm, d), lambda i: (i, 0)),
        out_shape=jax.ShapeDtypeStruct((r, d), F32),
        scratch_shapes=[pltpu.VMEM((tm, d), F32)],
        input_output_aliases={0: 0},
        compiler_params=_cparams(("arbitrary",)),
        name="post_mlp",
    )(*args)


def _dn_gate_layout(n_heads, hb):
    ng = n_heads // hb
    idx = np.full((ng, LANES), -1, np.int64)
    for g in range(ng):
        for kind in range(4):
            for j in range(hb):
                idx[g, kind * hb + j] = kind * n_heads + g * hb + j
    return idx.reshape(-1)


def kernel(x, c, ctx, c_ctx, ada_w, ada_b, norm_w, mlp_w1, mlp_w2, dn_w_in, dn_conv, dn_a_log, dn_dt_bias,
           dn_out_norm, dn_w_out, da_w_qkv, da_lambda, da_subln, da_w_out, final_norm):
    batch, lat_len, d = x.shape
    ctx_len = ctx.shape[1]
    depth = ada_w.shape[0]
    dn_width = dn_w_out.shape[1]
    dn_heads = dn_width // DN_HEAD_DIM
    da_v_width = da_w_out.shape[1]
    da_heads = da_v_width // DA_V_DIM
    da_qk_width = da_heads * 2 * DA_HEAD_DIM
    hb = DN_HEADS_PER_STEP

    n_rows = -(-(batch + 1) // SUBLANES) * SUBLANES
    cc = jnp.zeros((n_rows, d), F32).at[:batch].set(c).at[batch].set(c_ctx)
    mods = _modulation(cc, ada_w, ada_b).reshape(depth, n_rows, 6, d)
    lat_row = lambda b: b
    ctx_row = lambda b: batch

    xl = x.reshape(batch * lat_len, d)
    xc = ctx.reshape(batch * ctx_len, d)

    gate_idx = _dn_gate_layout(dn_heads, hb)
    gate_valid = jnp.asarray(gate_idx >= 0)
    gate_src = jnp.asarray(np.maximum(gate_idx, 0))
    ng = dn_heads // hb
    rope_tabs = _rope_tables(lat_len)

    for i in range(depth):
        last = i == depth - 1
        mods_l = mods[i]
        j = i // 2
        if i % 2 == 0:
            w_in = dn_w_in[j]
            w_main = w_in[:, :4 * dn_width].astype(BF16)
            w_gate = jnp.where(gate_valid[None, :], jnp.take(w_in[:, 4 * dn_width:], gate_src, axis=1), 0.0).astype(BF16)
            conv_w = jnp.zeros((SUBLANES, 3 * dn_width), F32).at[:DN_CONV_W].set(dn_conv[j])
            alog_g = jnp.zeros((ng, 1, LANES), F32).at[:, 0, :2 * hb].set(
                jnp.transpose(dn_a_log[j].reshape(2, ng, hb), (1, 0, 2)).reshape(ng, 2 * hb))
            dtb_g = jnp.zeros((ng, 1, LANES), F32).at[:, 0, :2 * hb].set(
                jnp.transpose(dn_dt_bias[j].reshape(2, ng, hb), (1, 0, 2)).reshape(ng, 2 * hb))
            nw0 = norm_w[i, 0].reshape(1, d)
            qkv_l, z_l, g_l = _dn_proj(xl, lat_len, lat_row, nw0, mods_l, w_main, w_gate)
            qkv_c, z_c, g_c = _dn_proj(xc, ctx_len, ctx_row, nw0, mods_l, w_main, w_gate)
            y_c, y_l = _dn_core(qkv_c, qkv_l, z_c, z_l, g_c, g_l, conv_w, alog_g, dtb_g,
                                dn_out_norm[j].reshape(1, DN_HEAD_DIM), batch, ctx_len, lat_len)
            w_out = dn_w_out[j].astype(BF16)
        else:
            lambda_init = 0.8 - 0.6 * math.exp(-0.3 * i)
            w_qkv = da_w_qkv[j].astype(BF16)
            nw0 = norm_w[i, 0].reshape(1, d)
            q_l, k_l, v_l = _da_proj(xl, lat_len, lat_row, nw0, mods_l, w_qkv, rope_tabs, da_qk_width, da_v_width)
            q_c, k_c, v_c = _da_proj(xc, ctx_len, ctx_row, nw0, mods_l, w_qkv, None, da_qk_width, da_v_width)
            lam = da_lambda[j]
            subln = da_subln[j].reshape(1, DA_V_DIM)
            y_l = _flash(q_l, (k_c, k_l), (v_c, v_l), (ctx_len, lat_len), lat_len, lam, subln, lambda_init,
                         batch, da_heads)
            y_c = None if last else _flash(q_c, (k_c,), (v_c,), (ctx_len,), ctx_len, lam, subln, lambda_init,
                                           batch, da_heads)
            w_out = da_w_out[j].astype(BF16)
        nw1 = norm_w[i, 1].reshape(1, d)
        w1 = mlp_w1[i].astype(BF16)
        w2 = mlp_w2[i].astype(BF16)
        xl = _post(xl, y_l, lat_len, lat_row, mods_l, w_out, nw1, w1, w2,
                   final_norm.reshape(1, d) if last else None)
        if not last:
            xc = _post(xc, y_c, ctx_len, ctx_row, mods_l, w_out, nw1, w1, w2, None)
    return xl.reshape(batch, lat_len, d)
```

```python
import functools
import math

import jax
import jax.numpy as jnp
import numpy as np
from jax import lax
from jax.experimental import pallas as pl
from jax.experimental.pallas import tpu as pltpu

F32 = jnp.float32
BF16 = jnp.bfloat16

NORM_EPS = 1e-6
ROPE_THETA = 10000.0
GRID_W = 64
LANES = 128
SUBLANES = 8
VMEM_LIMIT = 56 * 1024 * 1024

DN_HEAD_DIM = 128
DN_CONV_W = 5
DN_CHUNK = 128
DN_BASE = 16
DN_HEADS_PER_STEP = 2
DN_LOCAL_CHUNKS_PER_ITER = 3
DA_HEAD_DIM = 64
DA_V_DIM = 128
LOG2_E = 1.4426950408889634


def _dot(a, b):
    return jnp.dot(a, b, preferred_element_type=F32)


def _mm(a, b):
    return jnp.dot(a.astype(BF16), b.astype(BF16), preferred_element_type=F32)


def _silu(x):
    return x * jax.nn.sigmoid(x)


def _cparams(sem):
    return pltpu.CompilerParams(dimension_semantics=sem, vmem_limit_bytes=VMEM_LIMIT)


def _const_spec(shape):
    nd = len(shape)
    return pl.BlockSpec(shape, lambda *_: (0,) * nd, pipeline_mode=pl.Buffered(1))


def _mod_kernel(c_ref, w_ref, b_ref, o_ref):
    s = _silu(c_ref[...]).astype(BF16)
    o_ref[...] = _dot(s, w_ref[...].astype(BF16)) + b_ref[...]


def _modulation(cc, ada_w, ada_b):
    depth, d, n6 = ada_w.shape
    r = cc.shape[0]
    tn = n6 // 4
    return pl.pallas_call(
        _mod_kernel,
        grid=(depth, n6 // tn),
        in_specs=[
            pl.BlockSpec((r, d), lambda l, j: (0, 0)),
            pl.BlockSpec((None, d, tn), lambda l, j: (l, 0, j)),
            pl.BlockSpec((None, 1, tn), lambda l, j: (l, 0, j)),
        ],
        out_specs=pl.BlockSpec((None, r, tn), lambda l, j: (l, 0, j)),
        out_shape=jax.ShapeDtypeStruct((depth, r, n6), F32),
        compiler_params=_cparams(("arbitrary", "arbitrary")),
        name="adaln_mod",
    )(cc, ada_w, ada_b.reshape(depth, 1, n6))


def _norm_mod(x, nw, mod, shift_idx, scale_idx):
    y = x * lax.rsqrt(jnp.mean(x * x, axis=-1, keepdims=True) + NORM_EPS) * nw
    return y * (1.0 + mod[scale_idx:scale_idx + 1, :]) + mod[shift_idx:shift_idx + 1, :]


def _row_tile(rows_per_batch):
    tm = min(512, rows_per_batch)
    assert rows_per_batch % tm == 0
    return tm


def _dn_proj_kernel(x_ref, nw_ref, mod_ref, wm_ref, wg_ref, qkv_ref, z_ref, g_ref, *, width):
    h = _norm_mod(x_ref[...], nw_ref[...], mod_ref[...], 0, 1).astype(BF16)
    tn = 512
    for j in range(0, 3 * width, tn):
        qkv_ref[:, j:j + tn] = _dot(h, wm_ref[:, j:j + tn])
    for j in range(0, width, tn):
        z_ref[:, j:j + tn] = _dot(h, wm_ref[:, 3 * width + j:3 * width + j + tn])
    g_ref[...] = _dot(h, wg_ref[...])


def _dn_proj(x2d, rows_per_batch, mod_row, nw, mods_l, w_main, w_gate):
    r, d = x2d.shape
    width = w_main.shape[1] // 4
    gw = w_gate.shape[1]
    tm = _row_tile(rows_per_batch)
    per = rows_per_batch // tm
    return pl.pallas_call(
        functools.partial(_dn_proj_kernel, width=width),
        grid=(r // tm,),
        in_specs=[
            pl.BlockSpec((tm, d), lambda i: (i, 0)),
            _const_spec((1, d)),
            pl.BlockSpec((None, 6, d), lambda i: (mod_row(i // per), 0, 0)),
            _const_spec(w_main.shape),
            _const_spec(w_gate.shape),
        ],
        out_specs=[
            pl.BlockSpec((tm, 3 * width), lambda i: (i, 0)),
            pl.BlockSpec((tm, width), lambda i: (i, 0)),
            pl.BlockSpec((tm, gw), lambda i: (i, 0)),
        ],
        out_shape=[
            jax.ShapeDtypeStruct((r, 3 * width), F32),
            jax.ShapeDtypeStruct((r, width), F32),
            jax.ShapeDtypeStruct((r, gw), F32),
        ],
        compiler_params=_cparams(("arbitrary",)),
        name="dn_proj",
    )(x2d, nw, mods_l, w_main, w_gate)


def _cumsum_lanes(x, reverse):
    n = x.shape[1]
    lane = lax.broadcasted_iota(jnp.int32, x.shape, 1)
    s = 1
    while s < n:
        if reverse:
            x = x + jnp.where(lane < n - s, pltpu.roll(x, n - s, 1), 0.0)
        else:
            x = x + jnp.where(lane >= s, pltpu.roll(x, s, 1), 0.0)
        s *= 2
    return x


def _unit_tri_inverse_minus_identity(mats):
    n = mats[0].shape[0]
    row = lax.broadcasted_iota(jnp.int32, (n, n), 0)
    col = lax.broadcasted_iota(jnp.int32, (n, n), 1)

    def same_block(size):
        return (row // size) == (col // size)

    base = same_block(DN_BASE)
    accs = [jnp.where(base, -a, 0.0) for a in mats]
    steps = int(math.log2(DN_BASE))
    ps = [_mm(p, p) for p in accs]
    for k in range(1, steps):
        if k + 1 < steps:
            both = [_mm(p, jnp.concatenate([acc, p], axis=1)) for acc, p in zip(accs, ps)]
            accs = [acc + p + b[:, :n] for acc, p, b in zip(accs, ps, both)]
            ps = [b[:, n:] for b in both]
        else:
            accs = [acc + p + _mm(p, acc) for acc, p in zip(accs, ps)]
    size = DN_BASE
    while size < n:
        sel = same_block(2 * size) & jnp.logical_not(same_block(size))
        offs = [jnp.where(sel, a, 0.0) for a in mats]
        ys = [off + _mm(acc, off) for acc, off in zip(accs, offs)]
        accs = [acc - y - _mm(y, acc) for acc, y in zip(accs, ys)]
        size *= 2
    return accs


def _dn_core_kernel(qc_ref, kc_ref, vc_ref, ql_ref, kl_ref, vl_ref, zc_ref, zl_ref, gc_ref, gl_ref,
                    cwq_ref, cwk_ref, cwv_ref, alog_ref, dtb_ref, onorm_ref,
                    yc_ref, yl_ref,
                    qn_ref, kn_ref, vn_ref, knt_ref, gb_ref, gbt_ref,
                    u_ref, w_ref, qkm_ref, st_ref, *, hb, ctx_len, lat_len):
    C = DN_CHUNK
    seq = ctx_len + lat_len
    n_ctx, n_chunks = ctx_len // C, seq // C
    width = hb * LANES

    assert 4 * hb <= SUBLANES
    raws = []
    for c in range(n_chunks):
        src = gc_ref[c * C:(c + 1) * C, :] if c < n_ctx else gl_ref[(c - n_ctx) * C:(c - n_ctx + 1) * C, :]
        raws.append(src.T[0:SUBLANES, :])
    raw = jnp.concatenate(raws, axis=0)
    grow_id = lax.broadcasted_iota(jnp.int32, raw.shape, 0) % SUBLANES
    a = raw + jnp.concatenate([dtb_ref[...]] * n_chunks, axis=0)
    g = (jnp.concatenate([-jnp.exp(alog_ref[...])] * n_chunks, axis=0)
         * (jnp.maximum(a, 0.0) + jnp.log1p(jnp.exp(-jnp.abs(a)))))
    pre = _cumsum_lanes(g, reverse=False)
    suf = _cumsum_lanes(g, reverse=True)
    vals = jnp.where(grow_id < hb, pre, jnp.where(grow_id < 2 * hb, suf, jax.nn.sigmoid(raw)))
    zero_rows = jnp.zeros((C - SUBLANES, C), F32)
    for c in range(n_chunks):
        tile = vals[c * SUBLANES:(c + 1) * SUBLANES, :]
        gbt_ref[:, c * C:(c + 1) * C] = tile
        gb_ref[c * C:(c + 1) * C, :] = jnp.concatenate([tile, zero_rows], axis=0).T

    conv_parts = ((qc_ref, ql_ref, cwq_ref, qn_ref), (kc_ref, kl_ref, cwk_ref, kn_ref), (vc_ref, vl_ref, cwv_ref, vn_ref))
    halo = SUBLANES

    def conv_chunk(c, static):
        if static:
            in_ctx = c < n_ctx
            s0 = c * C if in_ctx else (c - n_ctx) * C
            seg_len = ctx_len if in_ctx else lat_len
            top_zero, bot_zero = s0 == 0, s0 + C == seg_len
            top_start, bot_start, dst = s0 - halo, s0 + C, c * C
        else:
            in_ctx, seg_len = False, lat_len
            s0 = pl.multiple_of((c - n_ctx) * C, C)
            top_zero, bot_zero = s0 == 0, s0 + C == seg_len
            top_start = pl.multiple_of(jnp.maximum(s0 - halo, 0), halo)
            bot_start = pl.multiple_of(jnp.minimum(s0 + C, seg_len - halo), halo)
            dst = pl.multiple_of(c * C, C)
        zeros_halo = jnp.zeros((halo, LANES), F32)
        for part, (c_ref, l_ref, cw_ref, dst_ref) in enumerate(conv_parts):
            src_ref = c_ref if in_ctx else l_ref
            for j in range(hb):
                cols = slice(j * LANES, (j + 1) * LANES)
                if static:
                    top = zeros_halo if top_zero else src_ref[pl.ds(top_start, halo), cols]
                    bot = zeros_halo if bot_zero else src_ref[pl.ds(bot_start, halo), cols]
                else:
                    top = jnp.where(top_zero, 0.0, src_ref[pl.ds(top_start, halo), cols])
                    bot = jnp.where(bot_zero, 0.0, src_ref[pl.ds(bot_start, halo), cols])
                win = jnp.concatenate([top, src_ref[pl.ds(s0, C), cols], bot], axis=0)
                acc = None
                for d in range(DN_CONV_W):
                    shift = d - DN_CONV_W // 2
                    moved = win if shift == 0 else pltpu.roll(win, (-shift) % win.shape[0], 0)
                    term = cw_ref[d:d + 1, cols] * moved[halo:halo + C, :]
                    acc = term if acc is None else acc + term
                y = _silu(acc)
                if part < 2:
                    y = y * lax.rsqrt(jnp.sum(y * y, axis=-1, keepdims=True) + 1e-6)
                if part == 0:
                    y = y * (DN_HEAD_DIM ** -0.5)
                dst_ref[pl.ds(dst, C), cols] = y
                if part == 1:
                    knt_ref[cols, pl.ds(dst, C)] = y.T

    row = lax.broadcasted_iota(jnp.int32, (C, C), 0)
    col = lax.broadcasted_iota(jnp.int32, (C, C), 1)

    def local_body(it, carry):
        chains, a_mats, rhss = [], [], []
        for sub in range(per_iter):
            r = pl.multiple_of((it * per_iter + sub) * C, C)
            gb = gb_ref[pl.ds(r, C), :]
            gbt = gbt_ref[:, pl.ds(r, C)]
            for j in range(hb):
                cols = slice(j * LANES, (j + 1) * LANES)
                qn = qn_ref[pl.ds(r, C), cols]
                kn = kn_ref[pl.ds(r, C), cols]
                vn = vn_ref[pl.ds(r, C), cols]
                knt = knt_ref[cols, pl.ds(r, C)].astype(BF16)
                kk = _dot(kn.astype(BF16), knt)
                qk = _dot(qn.astype(BF16), knt)
                for dr in range(2):
                    gl_, bl_ = dr * hb + j, 2 * hb + dr * hb + j
                    gcol, grow, bcol = gb[:, gl_:gl_ + 1], gbt[gl_:gl_ + 1, :], gb[:, bl_:bl_ + 1]
                    incl = (row >= col) if dr == 0 else (row <= col)
                    strict = (row > col) if dr == 0 else (row < col)
                    dec = jnp.where(incl, jnp.exp(jnp.where(incl, gcol - grow, 0.0)), 0.0)
                    qkm_ref[dr, pl.ds(r, C), cols] = (qk * dec).astype(BF16)
                    chains.append((dr, r, cols))
                    a_mats.append(jnp.where(strict, kk * dec * bcol, 0.0))
                    rhss.append(jnp.concatenate([vn * bcol, kn * (bcol * jnp.exp(gcol))], axis=1))
        tinvs = _unit_tri_inverse_minus_identity(a_mats)
        uws = [rhs + _mm(tinv, rhs) for tinv, rhs in zip(tinvs, rhss)]
        for (dr, r, cols), uw in zip(chains, uws):
            u_ref[dr, pl.ds(r, C), cols] = uw[:, :LANES]
            w_ref[dr, pl.ds(r, C), cols] = uw[:, LANES:].astype(BF16)
        return carry

    per_iter = max(p for p in range(1, DN_LOCAL_CHUNKS_PER_ITER + 1) if n_chunks % p == 0)
    n_groups = n_chunks // per_iter
    assert n_ctx <= per_iter
    for c in range(per_iter):
        conv_chunk(c, static=True)

    def group_body(it, carry):
        local_body(it, carry)
        for sub in range(per_iter):
            conv_chunk((it + 1) * per_iter + sub, static=False)
        return carry

    lax.fori_loop(0, n_groups - 1, group_body, 0)
    local_body(n_groups - 1, 0)

    st_ref[...] = jnp.zeros(st_ref.shape, F32)
    vn_ref[...] = jnp.zeros((seq, width), F32)

    def scan_body(s, carry):
        chains = []
        for dr in range(2):
            if dr == 0:
                c = s
            else:
                c = jnp.where(s < n_ctx, n_ctx - 1 - s, n_chunks - 1 - s + n_ctx)
            r = pl.multiple_of(c * C, C)
            gb = gb_ref[pl.ds(r, C), :]
            gbt = gbt_ref[:, pl.ds(r, C)]
            for j in range(hb):
                cols = slice(j * LANES, (j + 1) * LANES)
                gl_ = dr * hb + j
                gcol, grow = gb[:, gl_:gl_ + 1], gbt[gl_:gl_ + 1, :]
                g_end_c = gcol[C - 1:C, :] if dr == 0 else gcol[0:1, :]
                g_end_r = grow[:, C - 1:C] if dr == 0 else grow[:, 0:1]
                qd = (qn_ref[pl.ds(r, C), cols] * jnp.exp(gcol)).astype(BF16)
                kdt = (knt_ref[cols, pl.ds(r, C)] * jnp.exp(g_end_r - grow)).astype(BF16)
                chains.append((dr, j, r, cols, qd, kdt, jnp.exp(g_end_c)))
        states = [st_ref[dr * hb + j] for dr, j, *_ in chains]
        pss = [_dot(jnp.concatenate([w_ref[dr, pl.ds(r, C), cols], qd], axis=0), st.astype(BF16))
               for (dr, j, r, cols, qd, kdt, gt), st in zip(chains, states)]
        nus = [(u_ref[dr, pl.ds(r, C), cols] - ps[:C]).astype(BF16)
               for (dr, j, r, cols, qd, kdt, gt), ps in zip(chains, pss)]
        upds = [_dot(kdt, nu) for (dr, j, r, cols, qd, kdt, gt), nu in zip(chains, nus)]
        outs = [ps[C:] + _dot(qkm_ref[dr, pl.ds(r, C), cols], nu)
                for (dr, j, r, cols, qd, kdt, gt), ps, nu in zip(chains, pss, nus)]
        for (dr, j, r, cols, qd, kdt, gt), st, upd, out in zip(chains, states, upds, outs):
            st_ref[dr * hb + j] = st * gt + upd
            vn_ref[pl.ds(r, C), cols] += out
        return carry

    lax.fori_loop(0, n_chunks, scan_body, 0)

    onorm = onorm_ref[...]
    tr = 256
    for r0 in range(0, seq, tr):
        for j in range(hb):
            cols = slice(j * LANES, (j + 1) * LANES)
            o = vn_ref[r0:r0 + tr, cols]
            y = o * lax.rsqrt(jnp.mean(o * o, axis=-1, keepdims=True) + NORM_EPS) * onorm
            if r0 < ctx_len:
                yc_ref[r0:r0 + tr, cols] = (y * _silu(zc_ref[r0:r0 + tr, cols])).astype(BF16)
            else:
                q0 = r0 - ctx_len
                yl_ref[q0:q0 + tr, cols] = (y * _silu(zl_ref[q0:q0 + tr, cols])).astype(BF16)


def _dn_core(qkv_c, qkv_l, z_c, z_l, g_c, g_l, conv_w, alog_g, dtb_g, onorm, batch, ctx_len, lat_len):
    hb = DN_HEADS_PER_STEP
    width = hb * LANES
    dn_width = z_c.shape[1]
    ng = dn_width // width
    seq = ctx_len + lat_len
    assert ctx_len % 256 == 0 and lat_len % 256 == 0 and ctx_len % DN_CHUNK == 0

    def col_spec(rows, off):
        return pl.BlockSpec((rows, width), lambda b, g: (b, off + g))

    in_specs = [
        col_spec(ctx_len, 0), col_spec(ctx_len, ng), col_spec(ctx_len, 2 * ng),
        col_spec(lat_len, 0), col_spec(lat_len, ng), col_spec(lat_len, 2 * ng),
        col_spec(ctx_len, 0), col_spec(lat_len, 0),
        pl.BlockSpec((ctx_len, LANES), lambda b, g: (b, g)),
        pl.BlockSpec((lat_len, LANES), lambda b, g: (b, g)),
        pl.BlockSpec((SUBLANES, width), lambda b, g: (0, g)),
        pl.BlockSpec((SUBLANES, width), lambda b, g: (0, ng + g)),
        pl.BlockSpec((SUBLANES, width), lambda b, g: (0, 2 * ng + g)),
        pl.BlockSpec((None, SUBLANES, LANES), lambda b, g: (g, 0, 0)),
        pl.BlockSpec((None, SUBLANES, LANES), lambda b, g: (g, 0, 0)),
        pl.BlockSpec((1, LANES), lambda b, g: (0, 0)),
    ]
    scratch = [
        pltpu.VMEM((seq, width), F32),
        pltpu.VMEM((seq, width), F32),
        pltpu.VMEM((seq, width), F32),
        pltpu.VMEM((width, seq), F32),
        pltpu.VMEM((seq, LANES), F32),
        pltpu.VMEM((SUBLANES, seq), F32),
        pltpu.VMEM((2, seq, width), F32),
        pltpu.VMEM((2, seq, width), BF16),
        pltpu.VMEM((2, seq, width), BF16),
        pltpu.VMEM((2 * hb, DN_HEAD_DIM, DN_HEAD_DIM), F32),
    ]
    return pl.pallas_call(
        functools.partial(_dn_core_kernel, hb=hb, ctx_len=ctx_len, lat_len=lat_len),
        grid=(batch, ng),
        in_specs=in_specs,
        out_specs=[col_spec(ctx_len, 0), col_spec(lat_len, 0)],
        out_shape=[jax.ShapeDtypeStruct((batch * ctx_len, dn_width), BF16),
                   jax.ShapeDtypeStruct((batch * lat_len, dn_width), BF16)],
        scratch_shapes=scratch,
        compiler_params=_cparams(("arbitrary", "arbitrary")),
        name="dn_core",
    )(qkv_c, qkv_c, qkv_c, qkv_l, qkv_l, qkv_l, z_c, z_l, g_c, g_l, conv_w, conv_w, conv_w, alog_g, dtb_g, onorm)


def _da_proj_kernel(*refs, rope, qk_width, v_width):
    if rope:
        x_ref, nw_ref, mod_ref, w_ref, cos_ref, sin_ref, q_ref, k_ref, v_ref = refs
    else:
        x_ref, nw_ref, mod_ref, w_ref, q_ref, k_ref, v_ref = refs
    h = _norm_mod(x_ref[...], nw_ref[...], mod_ref[...], 0, 1).astype(BF16)
    tm = h.shape[0]
    if rope:
        cos, sin = cos_ref[...], sin_ref[...]
        lane = lax.broadcasted_iota(jnp.int32, (tm, LANES), 1)
        first_half = (lane % DA_HEAD_DIM) < (DA_HEAD_DIM // 2)
    tn = 512
    for which, dst in ((0, q_ref), (1, k_ref)):
        for j0 in range(0, qk_width, tn):
            wide = _dot(h, w_ref[:, which * qk_width + j0:which * qk_width + j0 + tn])
            for j in range(0, tn, LANES):
                y = wide[:, j:j + LANES]
                if which == 0:
                    y = y * (DA_HEAD_DIM ** -0.5 * LOG2_E)
                if rope:
                    half = DA_HEAD_DIM // 2
                    partner = jnp.where(first_half, pltpu.roll(y, LANES - half, 1), pltpu.roll(y, half, 1))
                    y = y * cos + partner * sin
                dst[:, j0 + j:j0 + j + LANES] = y.astype(BF16)
    for j in range(0, v_width, 512):
        v_ref[:, j:j + 512] = _dot(h, w_ref[:, 2 * qk_width + j:2 * qk_width + j + 512]).astype(BF16)


def _da_proj(x2d, rows_per_batch, mod_row, nw, mods_l, w_qkv, rope_tabs, qk_width, v_width):
    r, d = x2d.shape
    tm = _row_tile(rows_per_batch)
    per = rows_per_batch // tm
    in_specs = [
        pl.BlockSpec((tm, d), lambda i: (i, 0)),
        _const_spec((1, d)),
        pl.BlockSpec((None, 6, d), lambda i: (mod_row(i // per), 0, 0)),
        _const_spec(w_qkv.shape),
    ]
    args = [x2d, nw, mods_l, w_qkv]
    if rope_tabs is not None:
        in_specs += [pl.BlockSpec((tm, LANES), lambda i: (i % per, 0))] * 2
        args += list(rope_tabs)
    return pl.pallas_call(
        functools.partial(_da_proj_kernel, rope=rope_tabs is not None, qk_width=qk_width, v_width=v_width),
        grid=(r // tm,),
        in_specs=in_specs,
        out_specs=[pl.BlockSpec((tm, qk_width), lambda i: (i, 0)),
                   pl.BlockSpec((tm, qk_width), lambda i: (i, 0)),
                   pl.BlockSpec((tm, v_width), lambda i: (i, 0))],
        out_shape=[jax.ShapeDtypeStruct((r, qk_width), BF16),
                   jax.ShapeDtypeStruct((r, qk_width), BF16),
                   jax.ShapeDtypeStruct((r, v_width), BF16)],
        compiler_params=_cparams(("arbitrary",)),
        name="da_proj",
    )(*args)


def _rope_tables(n_tokens):
    rows = n_tokens // GRID_W
    row = jnp.repeat(jnp.arange(rows, dtype=jnp.int32), GRID_W).astype(F32)
    col = jnp.tile(jnp.arange(GRID_W, dtype=jnp.int32), rows).astype(F32)
    n_freq = DA_HEAD_DIM // 4
    inv = ROPE_THETA ** (-jnp.arange(n_freq, dtype=F32) / n_freq)
    ang = jnp.concatenate([row[:, None] * inv, col[:, None] * inv], axis=-1)
    cos, sin = jnp.cos(ang), jnp.sin(ang)
    reps = LANES // DA_HEAD_DIM
    return (jnp.tile(jnp.concatenate([cos, cos], axis=-1), (1, reps)),
            jnp.tile(jnp.concatenate([-sin, sin], axis=-1), (1, reps)))


def _flash_kernel(*refs, n_seg, seg_lens, tk, sub_rows, lambda_init):
    q_ref = refs[0]
    k_refs = refs[1:1 + n_seg]
    v_refs = refs[1 + n_seg:1 + 2 * n_seg]
    lam_ref, subln_ref, y_ref, s_ref, mx_ref, sum_ref = refs[1 + 2 * n_seg:]
    tq = q_ref.shape[0]
    lam = lam_ref[...]
    lam_full = (jnp.exp(jnp.sum(lam[0:1, :] * lam[1:2, :], axis=-1, keepdims=True))
                - jnp.exp(jnp.sum(lam[2:3, :] * lam[3:4, :], axis=-1, keepdims=True)) + lambda_init)
    lane = lax.broadcasted_iota(jnp.int32, (1, LANES), 1)
    map_masks = (lane < DA_HEAD_DIM, lane >= DA_HEAD_DIM)
    tiles = [(sg, r) for sg in range(n_seg) for r in range(0, seg_lens[sg], tk)]
    halves = [slice(h * LANES, (h + 1) * LANES) for h in range(tk // LANES)]
    sub = min(sub_rows, tq)

    def pass_a(rows):
        q = q_ref[rows, :]
        for mp in range(2):
            qz = jnp.where(map_masks[mp], q, jnp.zeros_like(q))
            run_max = None
            for t, (sg, r) in enumerate(tiles):
                k = k_refs[sg][r:r + tk, :]
                s = lax.dot_general(qz, k, (((1,), (1,)), ((), ())), preferred_element_type=F32)
                s_ref[mp, t, rows, :] = s
                for hs in halves:
                    run_max = s[:, hs] if run_max is None else jnp.maximum(run_max, s[:, hs])
            mx_ref[mp, rows, :] = jnp.broadcast_to(jnp.max(run_max, axis=-1, keepdims=True), (sub, LANES))

    def pass_b(rows):
        rb = 64
        for mp in range(2):
            for r0 in range(rows.start, rows.stop, rb):
                blk = slice(r0, r0 + rb)
                row_max = jnp.concatenate([mx_ref[mp, blk, :]] * len(halves), axis=1)
                run_sum = None
                for t in range(len(tiles)):
                    e = jnp.exp2(s_ref[mp, t, blk, :] - row_max)
                    s_ref[mp, t, blk, :] = e
                    for hs in halves:
                        run_sum = e[:, hs] if run_sum is None else run_sum + e[:, hs]
                total = jnp.sum(run_sum, axis=-1, keepdims=True)
                coef = (1.0 / total) if mp == 0 else (lam_full / total)
                sum_ref[mp, blk, :] = jnp.broadcast_to(coef, (rb, LANES))

    def pass_c(rows):
        c0 = jnp.concatenate([sum_ref[0, rows, :]] * len(halves), axis=1)
        c1 = jnp.concatenate([sum_ref[1, rows, :]] * len(halves), axis=1)
        acc = None
        for t, (sg, r) in enumerate(tiles):
            p = (s_ref[0, t, rows, :] * c0 - s_ref[1, t, rows, :] * c1).astype(BF16)
            part = _dot(p, v_refs[sg][r:r + tk, :])
            acc = part if acc is None else acc + part
        y = acc * lax.rsqrt(jnp.mean(acc * acc, axis=-1, keepdims=True) + NORM_EPS) * subln_ref[...]
        y_ref[rows, :] = (y * (1.0 - lambda_init)).astype(BF16)

    subs = [slice(r0, r0 + sub) for r0 in range(0, tq, sub)]
    pass_a(subs[0])
    for u in range(1, len(subs)):
        pass_a(subs[u])
        pass_b(subs[u - 1])
        pass_c(subs[u - 1])
    pass_b(subs[-1])
    pass_c(subs[-1])


def _flash(q, ks, vs, seg_lens, q_rows_per_batch, lam, subln, lambda_init, batch, heads):
    tq = min(1024, q_rows_per_batch)
    tk = 256
    nq = q_rows_per_batch // tq
    n_seg = len(ks)
    n_tiles = sum(n // tk for n in seg_lens)
    assert all(n % tk == 0 for n in seg_lens)
    in_specs = [pl.BlockSpec((tq, LANES), lambda b, h, i: (b * nq + i, h))]
    for n in list(seg_lens) * 2:
        in_specs.append(pl.BlockSpec((n, LANES), lambda b, h, i: (b, h)))
    in_specs += [pl.BlockSpec(lam.shape, lambda b, h, i: (0, 0)), pl.BlockSpec((1, LANES), lambda b, h, i: (0, 0))]
    return pl.pallas_call(
        functools.partial(_flash_kernel, n_seg=n_seg, seg_lens=tuple(seg_lens), tk=tk, sub_rows=256,
                          lambda_init=lambda_init),
        grid=(batch, heads, nq),
        in_specs=in_specs,
        out_specs=pl.BlockSpec((tq, LANES), lambda b, h, i: (b * nq + i, h)),
        out_shape=jax.ShapeDtypeStruct((q.shape[0], heads * DA_V_DIM), BF16),
        scratch_shapes=[pltpu.VMEM((2, n_tiles, tq, tk), F32),
                        pltpu.VMEM((2, tq, LANES), F32),
                        pltpu.VMEM((2, tq, LANES), F32)],
        compiler_params=_cparams(("arbitrary", "arbitrary", "arbitrary")),
        name="diff_flash",
    )(q, *ks, *vs, lam, subln)


def _post_kernel(*refs, final, ff_tile):
    if final:
        x_ref, y_ref, mod_ref, wo_ref, nw_ref, w1_ref, w2_ref, fw_ref, o_ref, acc_ref = refs
    else:
        x_ref, y_ref, mod_ref, wo_ref, nw_ref, w1_ref, w2_ref, o_ref, acc_ref = refs
    mod = mod_ref[...]
    x1 = x_ref[...] + mod[2:3, :] * _dot(y_ref[...], wo_ref[...])
    h = _norm_mod(x1, nw_ref[...], mod, 3, 4).astype(BF16)
    d_ff = w1_ref.shape[1]
    for f in range(0, d_ff, ff_tile):
        a = jnp.maximum(_dot(h, w1_ref[:, f:f + ff_tile]), 0.0)
        part = _dot((a * a).astype(BF16), w2_ref[f:f + ff_tile, :])
        if f == 0:
            acc_ref[...] = part
        else:
            acc_ref[...] += part
    x2 = x1 + mod[5:6, :] * acc_ref[...]
    if final:
        x2 = x2 * lax.rsqrt(jnp.mean(x2 * x2, axis=-1, keepdims=True) + NORM_EPS) * fw_ref[...]
    o_ref[...] = x2


def _post(x2d, y2d, rows_per_batch, mod_row, mods_l, w_out, nw, w1, w2, final_w):
    r, d = x2d.shape
    tm = _row_tile(rows_per_batch)
    per = rows_per_batch // tm
    final = final_w is not None
    in_specs = [
        pl.BlockSpec((tm, d), lambda i: (i, 0)),
        pl.BlockSpec((tm, y2d.shape[1]), lambda i: (i, 0)),
        pl.BlockSpec((None, 6, d), lambda i: (mod_row(i // per), 0, 0)),
        _const_spec(w_out.shape),
        _const_spec((1, d)),
        _const_spec(w1.shape),
        _const_spec(w2.shape),
    ]
    args = [x2d, y2d, mods_l, w_out, nw, w1, w2]
    if final:
        in_specs.append(_const_spec((1, d)))
        args.append(final_w)
    return pl.pallas_call(
        functools.partial(_post_kernel, final=final, ff_tile=1024),
        grid=(r // tm,),
        in_specs=in_specs,
        out_specs=pl.BlockSpec((tm, d), lambda i: (i, 0)),
        out_shape=jax.ShapeDtypeStruct((r, d), F32),
        scratch_shapes=[pltpu.VMEM((tm, d), F32)],
        compiler_params=_cparams(("arbitrary",)),
        name="post_mlp",
    )(*args)


def _dn_gate_layout(n_heads, hb):
    ng = n_heads // hb
    idx = np.full((ng, LANES), -1, np.int64)
    for g in range(ng):
        for kind in range(4):
            for j in range(hb):
                idx[g, kind * hb + j] = kind * n_heads + g * hb + j
    return idx.reshape(-1)


def kernel(x, c, ctx, c_ctx, ada_w, ada_b, norm_w, mlp_w1, mlp_w2, dn_w_in, dn_conv, dn_a_log, dn_dt_bias,
           dn_out_norm, dn_w_out, da_w_qkv, da_lambda, da_subln, da_w_out, final_norm):
    batch, lat_len, d = x.shape
    ctx_len = ctx.shape[1]
    depth = ada_w.shape[0]
    dn_width = dn_w_out.shape[1]
    dn_heads = dn_width // DN_HEAD_DIM
    da_v_width = da_w_out.shape[1]
    da_heads = da_v_width // DA_V_DIM
    da_qk_width = da_heads * 2 * DA_HEAD_DIM
    hb = DN_HEADS_PER_STEP

    n_rows = -(-(batch + 1) // SUBLANES) * SUBLANES
    cc = jnp.zeros((n_rows, d), F32).at[:batch].set(c).at[batch].set(c_ctx)
    mods = _modulation(cc, ada_w, ada_b).reshape(depth, n_rows, 6, d)
    lat_row = lambda b: b
    ctx_row = lambda b: batch

    xl = x.reshape(batch * lat_len, d)
    xc = ctx.reshape(batch * ctx_len, d)

    gate_idx = _dn_gate_layout(dn_heads, hb)
    gate_valid = jnp.asarray(gate_idx >= 0)
    gate_src = jnp.asarray(np.maximum(gate_idx, 0))
    ng = dn_heads // hb
    rope_tabs = _rope_tables(lat_len)

    for i in range(depth):
        last = i == depth - 1
        mods_l = mods[i]
        j = i // 2
        if i % 2 == 0:
            w_in = dn_w_in[j]
            w_main = w_in[:, :4 * dn_width].astype(BF16)
            w_gate = jnp.where(gate_valid[None, :], jnp.take(w_in[:, 4 * dn_width:], gate_src, axis=1), 0.0).astype(BF16)
            conv_w = jnp.zeros((SUBLANES, 3 * dn_width), F32).at[:DN_CONV_W].set(dn_conv[j])
            def gate_rows(p):
                rows = jnp.transpose(p.reshape(2, ng, hb), (1, 0, 2)).reshape(ng, 2 * hb, 1)
                return jnp.zeros((ng, SUBLANES, LANES), F32).at[:, :2 * hb, :].set(
                    jnp.broadcast_to(rows, (ng, 2 * hb, LANES)))

            alog_g, dtb_g = gate_rows(dn_a_log[j]), gate_rows(dn_dt_bias[j])
            nw0 = norm_w[i, 0].reshape(1, d)
            qkv_l, z_l, g_l = _dn_proj(xl, lat_len, lat_row, nw0, mods_l, w_main, w_gate)
            qkv_c, z_c, g_c = _dn_proj(xc, ctx_len, ctx_row, nw0, mods_l, w_main, w_gate)
            y_c, y_l = _dn_core(qkv_c, qkv_l, z_c, z_l, g_c, g_l, conv_w, alog_g, dtb_g,
                                dn_out_norm[j].reshape(1, DN_HEAD_DIM), batch, ctx_len, lat_len)
            w_out = dn_w_out[j].astype(BF16)
        else:
            lambda_init = 0.8 - 0.6 * math.exp(-0.3 * i)
            w_qkv = da_w_qkv[j].astype(BF16)
            nw0 = norm_w[i, 0].reshape(1, d)
            q_l, k_l, v_l = _da_proj(xl, lat_len, lat_row, nw0, mods_l, w_qkv, rope_tabs, da_qk_width, da_v_width)
            q_c, k_c, v_c = _da_proj(xc, ctx_len, ctx_row, nw0, mods_l, w_qkv, None, da_qk_width, da_v_width)
            lam = da_lambda[j]
            subln = da_subln[j].reshape(1, DA_V_DIM)
            y_l = _flash(q_l, (k_c, k_l), (v_c, v_l), (ctx_len, lat_len), lat_len, lam, subln, lambda_init,
                         batch, da_heads)
            y_c = None if last else _flash(q_c, (k_c,), (v_c,), (ctx_len,), ctx_len, lam, subln, lambda_init,
                                           batch, da_heads)
            w_out = da_w_out[j].astype(BF16)
        nw1 = norm_w[i, 1].reshape(1, d)
        w1 = mlp_w1[i].astype(BF16)
        w2 = mlp_w2[i].astype(BF16)
        xl = _post(xl, y_l, lat_len, lat_row, mods_l, w_out, nw1, w1, w2,
                   final_norm.reshape(1, d) if last else None)
        if not last:
            xc = _post(xc, y_c, ctx_len, ctx_row, mods_l, w_out, nw1, w1, w2, None)
    return xl.reshape(batch, lat_len, d)
```

```python
import functools
import math

import jax
import jax.numpy as jnp
import numpy as np
from jax import lax
from jax.experimental import pallas as pl
from jax.experimental.pallas import tpu as pltpu

F32 = jnp.float32
BF16 = jnp.bfloat16

NORM_EPS = 1e-6
ROPE_THETA = 10000.0
GRID_W = 64
LANES = 128
SUBLANES = 8
VMEM_LIMIT = 56 * 1024 * 1024

DN_HEAD_DIM = 128
DN_CONV_W = 5
DN_CHUNK = 128
DN_BASE = 16
DN_HEADS_PER_STEP = 2
DN_LOCAL_CHUNKS_PER_ITER = 3
DA_HEAD_DIM = 64
DA_V_DIM = 128
LOG2_E = 1.4426950408889634


def _dot(a, b):
    return jnp.dot(a, b, preferred_element_type=F32)


def _mm(a, b):
    return jnp.dot(a.astype(BF16), b.astype(BF16), preferred_element_type=F32)


def _silu(x):
    return x * jax.nn.sigmoid(x)


def _cparams(sem):
    return pltpu.CompilerParams(dimension_semantics=sem, vmem_limit_bytes=VMEM_LIMIT)


def _const_spec(shape):
    nd = len(shape)
    return pl.BlockSpec(shape, lambda *_: (0,) * nd, pipeline_mode=pl.Buffered(1))


def _mod_kernel(c_ref, w_ref, b_ref, o_ref):
    s = _silu(c_ref[...]).astype(BF16)
    o_ref[...] = _dot(s, w_ref[...].astype(BF16)) + b_ref[...]


def _modulation(cc, ada_w, ada_b):
    depth, d, n6 = ada_w.shape
    r = cc.shape[0]
    tn = n6 // 4
    return pl.pallas_call(
        _mod_kernel,
        grid=(depth, n6 // tn),
        in_specs=[
            pl.BlockSpec((r, d), lambda l, j: (0, 0)),
            pl.BlockSpec((None, d, tn), lambda l, j: (l, 0, j)),
            pl.BlockSpec((None, 1, tn), lambda l, j: (l, 0, j)),
        ],
        out_specs=pl.BlockSpec((None, r, tn), lambda l, j: (l, 0, j)),
        out_shape=jax.ShapeDtypeStruct((depth, r, n6), F32),
        compiler_params=_cparams(("arbitrary", "arbitrary")),
        name="adaln_mod",
    )(cc, ada_w, ada_b.reshape(depth, 1, n6))


def _norm_mod(x, nw, mod, shift_idx, scale_idx):
    y = x * lax.rsqrt(jnp.mean(x * x, axis=-1, keepdims=True) + NORM_EPS) * nw
    return y * (1.0 + mod[scale_idx:scale_idx + 1, :]) + mod[shift_idx:shift_idx + 1, :]


def _row_tile(rows_per_batch):
    tm = min(512, rows_per_batch)
    assert rows_per_batch % tm == 0
    return tm


DN_PROJ_HALO = 16


def _dn_proj_kernel(x_ref, xt_ref, xb_ref, nw_ref, mod_ref, wm_ref, wg_ref, cw_ref,
                    qkv_ref, kt_ref, z_ref, g_ref, *, width, per):
    halo = DN_PROJ_HALO
    tm = x_ref.shape[0]
    pos = pl.program_id(0) % per
    first, last = pos == 0, pos == per - 1
    xx = jnp.concatenate([xt_ref[...], x_ref[...], xb_ref[...]], axis=0)
    hh = _norm_mod(xx, nw_ref[...], mod_ref[...], 0, 1).astype(BF16)
    h = hh[halo:halo + tm, :]
    tn = 512
    for j0 in range(0, 3 * width, tn):
        wide = _dot(hh, wm_ref[:, j0:j0 + tn])
        wide = jnp.concatenate([jnp.where(first, 0.0, wide[0:halo, :]), wide[halo:halo + tm, :],
                                jnp.where(last, 0.0, wide[halo + tm:, :])], axis=0)
        for j in range(j0, j0 + tn, LANES):
            win = wide[:, j - j0:j - j0 + LANES]
            acc = None
            for d in range(DN_CONV_W):
                shift = d - DN_CONV_W // 2
                moved = win if shift == 0 else pltpu.roll(win, (-shift) % win.shape[0], 0)
                term = cw_ref[d:d + 1, j:j + LANES] * moved[halo:halo + tm, :]
                acc = term if acc is None else acc + term
            y = _silu(acc)
            if j < 2 * width:
                y = y * lax.rsqrt(jnp.sum(y * y, axis=-1, keepdims=True) + 1e-6)
            if j < width:
                y = y * (DN_HEAD_DIM ** -0.5)
            qkv_ref[:, j:j + LANES] = y
            if width <= j < 2 * width:
                kt_ref[j - width:j - width + LANES, :] = y.T
    for j in range(0, width, tn):
        z_ref[:, j:j + tn] = _dot(h, wm_ref[:, 3 * width + j:3 * width + j + tn])
    g_ref[...] = _dot(h, wg_ref[...])


def _dn_proj(x2d, rows_per_batch, mod_row, nw, mods_l, w_main, w_gate, conv_w):
    r, d = x2d.shape
    width = w_main.shape[1] // 4
    gw = w_gate.shape[1]
    tm = _row_tile(rows_per_batch)
    per = rows_per_batch // tm
    halo = DN_PROJ_HALO
    hpt = tm // halo
    return pl.pallas_call(
        functools.partial(_dn_proj_kernel, width=width, per=per),
        grid=(r // tm,),
        in_specs=[
            pl.BlockSpec((tm, d), lambda i: (i, 0)),
            pl.BlockSpec((halo, d), lambda i: (jnp.maximum(i * hpt - 1, 0), 0)),
            pl.BlockSpec((halo, d), lambda i: (jnp.minimum((i + 1) * hpt, r // halo - 1), 0)),
            _const_spec((1, d)),
            pl.BlockSpec((None, 6, d), lambda i: (mod_row(i // per), 0, 0)),
            _const_spec(w_main.shape),
            _const_spec(w_gate.shape),
            _const_spec(conv_w.shape),
        ],
        out_specs=[
            pl.BlockSpec((tm, 3 * width), lambda i: (i, 0)),
            pl.BlockSpec((width, tm), lambda i: (0, i)),
            pl.BlockSpec((tm, width), lambda i: (i, 0)),
            pl.BlockSpec((tm, gw), lambda i: (i, 0)),
        ],
        out_shape=[
            jax.ShapeDtypeStruct((r, 3 * width), F32),
            jax.ShapeDtypeStruct((width, r), F32),
            jax.ShapeDtypeStruct((r, width), F32),
            jax.ShapeDtypeStruct((r, gw), F32),
        ],
        compiler_params=_cparams(("arbitrary",)),
        name="dn_proj",
    )(x2d, x2d, x2d, nw, mods_l, w_main, w_gate, conv_w)


def _cumsum_lanes(x, reverse):
    n = x.shape[1]
    lane = lax.broadcasted_iota(jnp.int32, x.shape, 1)
    s = 1
    while s < n:
        if reverse:
            x = x + jnp.where(lane < n - s, pltpu.roll(x, n - s, 1), 0.0)
        else:
            x = x + jnp.where(lane >= s, pltpu.roll(x, s, 1), 0.0)
        s *= 2
    return x


def _unit_tri_inverse(mats):
    n = mats[0].shape[0]
    row = lax.broadcasted_iota(jnp.int32, (n, n), 0)
    col = lax.broadcasted_iota(jnp.int32, (n, n), 1)

    def same_block(size):
        return (row // size) == (col // size)

    base = same_block(DN_BASE)
    negs = [-a for a in mats]
    accs = [jnp.where(base, na, 0.0) for na in negs]
    steps = int(math.log2(DN_BASE))
    ps = [_mm(p, p) for p in accs]
    for k in range(1, steps):
        if k + 1 < steps:
            both = [_mm(p, jnp.concatenate([acc, p], axis=1)) for acc, p in zip(accs, ps)]
            accs = [acc + p + b[:, :n] for acc, p, b in zip(accs, ps, both)]
            ps = [b[:, n:] for b in both]
        else:
            accs = [acc + p + _mm(p, acc) for acc, p in zip(accs, ps)]
    xs = [jnp.where(row == col, 1.0, acc).astype(BF16) for acc in accs]
    size = DN_BASE
    while size < n:
        sel = same_block(2 * size) & jnp.logical_not(same_block(size))
        offs = [jnp.where(sel, na, 0.0).astype(BF16) for na in negs]
        ys = [_dot(x, off).astype(BF16) for x, off in zip(xs, offs)]
        xs = [x + _dot(y, x).astype(BF16) for x, y in zip(xs, ys)]
        size *= 2
    return xs


def _dn_core_kernel(qc_ref, kc_ref, vc_ref, ql_ref, kl_ref, vl_ref, ktc_ref, ktl_ref, zc_ref, zl_ref, gc_ref, gl_ref,
                    alog_ref, dtb_ref, onorm_ref,
                    yc_ref, yl_ref,
                    gb_ref, gbt_ref, u_ref, w_ref, qkm_ref, o_ref, st_ref, *, hb, ctx_len, lat_len):
    C = DN_CHUNK
    seq = ctx_len + lat_len
    n_ctx, n_chunks = ctx_len // C, seq // C
    width = hb * LANES

    assert 4 * hb <= SUBLANES
    raws = []
    for c in range(n_chunks):
        src = gc_ref[c * C:(c + 1) * C, :] if c < n_ctx else gl_ref[(c - n_ctx) * C:(c - n_ctx + 1) * C, :]
        raws.append(src.T[0:SUBLANES, :])
    raw = jnp.concatenate(raws, axis=0)
    grow_id = lax.broadcasted_iota(jnp.int32, raw.shape, 0) % SUBLANES
    a = raw + jnp.concatenate([dtb_ref[...]] * n_chunks, axis=0)
    g = (jnp.concatenate([-jnp.exp(alog_ref[...])] * n_chunks, axis=0)
         * (jnp.maximum(a, 0.0) + jnp.log1p(jnp.exp(-jnp.abs(a)))))
    pre = _cumsum_lanes(g, reverse=False)
    suf = _cumsum_lanes(g, reverse=True)
    vals = jnp.where(grow_id < hb, pre, jnp.where(grow_id < 2 * hb, suf, jax.nn.sigmoid(raw)))
    zero_rows = jnp.zeros((C - SUBLANES, C), F32)
    for c in range(n_chunks):
        tile = vals[c * SUBLANES:(c + 1) * SUBLANES, :]
        gbt_ref[:, c * C:(c + 1) * C] = tile
        gb_ref[c * C:(c + 1) * C, :] = jnp.concatenate([tile, zero_rows], axis=0).T

    def source(c):
        if isinstance(c, int) and c < n_ctx:
            return qc_ref, kc_ref, vc_ref, ktc_ref, c * C
        r = (c - n_ctx) * C
        return ql_ref, kl_ref, vl_ref, ktl_ref, (r if isinstance(c, int) else pl.multiple_of(r, C))

    def scratch_row(c):
        return c * C if isinstance(c, int) else pl.multiple_of(c * C, C)

    row = lax.broadcasted_iota(jnp.int32, (C, C), 0)
    col = lax.broadcasted_iota(jnp.int32, (C, C), 1)

    def local_chunks(chunk_ids):
        chains, a_mats, rhss = [], [], []
        for c in chunk_ids:
            q_ref, k_ref, v_ref, kt_ref, sr = source(c)
            r = scratch_row(c)
            gb = gb_ref[pl.ds(r, C), :]
            gbt = gbt_ref[:, pl.ds(r, C)]
            for j in range(hb):
                cols = slice(j * LANES, (j + 1) * LANES)
                qn = q_ref[pl.ds(sr, C), cols]
                kn = k_ref[pl.ds(sr, C), cols]
                vn = v_ref[pl.ds(sr, C), cols]
                knt = kt_ref[cols, pl.ds(sr, C)].astype(BF16)
                kk = _dot(kn.astype(BF16), knt)
                qk = _dot(qn.astype(BF16), knt)
                for dr in range(2):
                    gl_, bl_ = dr * hb + j, 2 * hb + dr * hb + j
                    gcol, grow, bcol = gb[:, gl_:gl_ + 1], gbt[gl_:gl_ + 1, :], gb[:, bl_:bl_ + 1]
                    incl = (row >= col) if dr == 0 else (row <= col)
                    strict = (row > col) if dr == 0 else (row < col)
                    dec = jnp.where(incl, jnp.exp(jnp.where(incl, gcol - grow, 0.0)), 0.0)
                    qkm_ref[dr, pl.ds(r, C), cols] = (qk * dec).astype(BF16)
                    chains.append((dr, r, cols))
                    a_mats.append(jnp.where(strict, kk * dec * bcol, 0.0))
                    rhss.append(jnp.concatenate([vn * bcol, kn * (bcol * jnp.exp(gcol))], axis=1))
        tinvs = _unit_tri_inverse(a_mats)
        uws = [_dot(tinv, rhs.astype(BF16)) for tinv, rhs in zip(tinvs, rhss)]
        for (dr, r, cols), uw in zip(chains, uws):
            u_ref[dr, pl.ds(r, C), cols] = uw[:, :LANES]
            w_ref[dr, pl.ds(r, C), cols] = uw[:, LANES:].astype(BF16)

    per_iter = max(p for p in range(1, DN_LOCAL_CHUNKS_PER_ITER + 1) if n_chunks % p == 0)
    n_groups = n_chunks // per_iter
    assert n_ctx <= per_iter
    local_chunks(list(range(per_iter)))

    def local_body(it, carry):
        local_chunks([it * per_iter + sub for sub in range(per_iter)])
        return carry

    lax.fori_loop(1, n_groups, local_body, 0)

    st_ref[...] = jnp.zeros(st_ref.shape, F32)
    o_ref[...] = jnp.zeros((seq, width), F32)

    def scan_step(s):
        chains = []
        for dr in range(2):
            if dr == 0:
                c = s
            elif isinstance(s, int):
                c = n_ctx - 1 - s if s < n_ctx else n_chunks - 1 - s + n_ctx
            else:
                c = n_chunks - 1 - s + n_ctx
            q_ref, _, _, kt_ref, sr = source(c)
            r = scratch_row(c)
            gb = gb_ref[pl.ds(r, C), :]
            gbt = gbt_ref[:, pl.ds(r, C)]
            for j in range(hb):
                cols = slice(j * LANES, (j + 1) * LANES)
                gl_ = dr * hb + j
                gcol, grow = gb[:, gl_:gl_ + 1], gbt[gl_:gl_ + 1, :]
                g_end_c = gcol[C - 1:C, :] if dr == 0 else gcol[0:1, :]
                g_end_r = grow[:, C - 1:C] if dr == 0 else grow[:, 0:1]
                qd = (q_ref[pl.ds(sr, C), cols] * jnp.exp(gcol)).astype(BF16)
                kdt = (kt_ref[cols, pl.ds(sr, C)] * jnp.exp(g_end_r - grow)).astype(BF16)
                chains.append((dr, j, r, cols, qd, kdt, jnp.exp(g_end_c)))
        states = [st_ref[dr * hb + j] for dr, j, *_ in chains]
        pss = [_dot(jnp.concatenate([w_ref[dr, pl.ds(r, C), cols], qd], axis=0), st.astype(BF16))
               for (dr, j, r, cols, qd, kdt, gt), st in zip(chains, states)]
        nus = [(u_ref[dr, pl.ds(r, C), cols] - ps[:C]).astype(BF16)
               for (dr, j, r, cols, qd, kdt, gt), ps in zip(chains, pss)]
        upds = [_dot(kdt, nu) for (dr, j, r, cols, qd, kdt, gt), nu in zip(chains, nus)]
        outs = [ps[C:] + _dot(qkm_ref[dr, pl.ds(r, C), cols], nu)
                for (dr, j, r, cols, qd, kdt, gt), ps, nu in zip(chains, pss, nus)]
        for (dr, j, r, cols, qd, kdt, gt), st, upd, out in zip(chains, states, upds, outs):
            st_ref[dr * hb + j] = st * gt + upd
            o_ref[pl.ds(r, C), cols] += out

    for s in range(n_ctx):
        scan_step(s)

    def scan_body(s, carry):
        scan_step(s)
        return carry

    lax.fori_loop(n_ctx, n_chunks, scan_body, 0)

    onorm = onorm_ref[...]
    tr = 256
    for r0 in range(0, seq, tr):
        for j in range(hb):
            cols = slice(j * LANES, (j + 1) * LANES)
            o = o_ref[r0:r0 + tr, cols]
            y = o * lax.rsqrt(jnp.mean(o * o, axis=-1, keepdims=True) + NORM_EPS) * onorm
            if r0 < ctx_len:
                yc_ref[r0:r0 + tr, cols] = (y * _silu(zc_ref[r0:r0 + tr, cols])).astype(BF16)
            else:
                q0 = r0 - ctx_len
                yl_ref[q0:q0 + tr, cols] = (y * _silu(zl_ref[q0:q0 + tr, cols])).astype(BF16)


def _dn_core(qkv_c, qkv_l, kt_c, kt_l, z_c, z_l, g_c, g_l, alog_g, dtb_g, onorm, batch, ctx_len, lat_len):
    hb = DN_HEADS_PER_STEP
    width = hb * LANES
    dn_width = z_c.shape[1]
    ng = dn_width // width
    seq = ctx_len + lat_len
    assert ctx_len % 256 == 0 and lat_len % 256 == 0 and ctx_len % DN_CHUNK == 0

    def col_spec(rows, off):
        return pl.BlockSpec((rows, width), lambda b, g: (b, off + g))

    in_specs = [
        col_spec(ctx_len, 0), col_spec(ctx_len, ng), col_spec(ctx_len, 2 * ng),
        col_spec(lat_len, 0), col_spec(lat_len, ng), col_spec(lat_len, 2 * ng),
        pl.BlockSpec((width, ctx_len), lambda b, g: (g, b)),
        pl.BlockSpec((width, lat_len), lambda b, g: (g, b)),
        col_spec(ctx_len, 0), col_spec(lat_len, 0),
        pl.BlockSpec((ctx_len, LANES), lambda b, g: (b, g)),
        pl.BlockSpec((lat_len, LANES), lambda b, g: (b, g)),
        pl.BlockSpec((None, SUBLANES, LANES), lambda b, g: (g, 0, 0)),
        pl.BlockSpec((None, SUBLANES, LANES), lambda b, g: (g, 0, 0)),
        pl.BlockSpec((1, LANES), lambda b, g: (0, 0)),
    ]
    scratch = [
        pltpu.VMEM((seq, LANES), F32),
        pltpu.VMEM((SUBLANES, seq), F32),
        pltpu.VMEM((2, seq, width), F32),
        pltpu.VMEM((2, seq, width), BF16),
        pltpu.VMEM((2, seq, width), BF16),
        pltpu.VMEM((seq, width), F32),
        pltpu.VMEM((2 * hb, DN_HEAD_DIM, DN_HEAD_DIM), F32),
    ]
    return pl.pallas_call(
        functools.partial(_dn_core_kernel, hb=hb, ctx_len=ctx_len, lat_len=lat_len),
        grid=(batch, ng),
        in_specs=in_specs,
        out_specs=[col_spec(ctx_len, 0), col_spec(lat_len, 0)],
        out_shape=[jax.ShapeDtypeStruct((batch * ctx_len, dn_width), BF16),
                   jax.ShapeDtypeStruct((batch * lat_len, dn_width), BF16)],
        scratch_shapes=scratch,
        compiler_params=_cparams(("arbitrary", "arbitrary")),
        name="dn_core",
    )(qkv_c, qkv_c, qkv_c, qkv_l, qkv_l, qkv_l, kt_c, kt_l, z_c, z_l, g_c, g_l, alog_g, dtb_g, onorm)


def _da_proj_kernel(*refs, rope, qk_width, v_width):
    if rope:
        x_ref, nw_ref, mod_ref, w_ref, cos_ref, sin_ref, q_ref, k_ref, v_ref = refs
    else:
        x_ref, nw_ref, mod_ref, w_ref, q_ref, k_ref, v_ref = refs
    h = _norm_mod(x_ref[...], nw_ref[...], mod_ref[...], 0, 1).astype(BF16)
    tm = h.shape[0]
    if rope:
        cos, sin = cos_ref[...], sin_ref[...]
        lane = lax.broadcasted_iota(jnp.int32, (tm, LANES), 1)
        first_half = (lane % DA_HEAD_DIM) < (DA_HEAD_DIM // 2)
    tn = 512
    for which, dst in ((0, q_ref), (1, k_ref)):
        for j0 in range(0, qk_width, tn):
            wide = _dot(h, w_ref[:, which * qk_width + j0:which * qk_width + j0 + tn])
            for j in range(0, tn, LANES):
                y = wide[:, j:j + LANES]
                if which == 0:
                    y = y * (DA_HEAD_DIM ** -0.5 * LOG2_E)
                if rope:
                    half = DA_HEAD_DIM // 2
                    partner = jnp.where(first_half, pltpu.roll(y, LANES - half, 1), pltpu.roll(y, half, 1))
                    y = y * cos + partner * sin
                dst[:, j0 + j:j0 + j + LANES] = y.astype(BF16)
    for j in range(0, v_width, 512):
        v_ref[:, j:j + 512] = _dot(h, w_ref[:, 2 * qk_width + j:2 * qk_width + j + 512]).astype(BF16)


def _da_proj(x2d, rows_per_batch, mod_row, nw, mods_l, w_qkv, rope_tabs, qk_width, v_width):
    r, d = x2d.shape
    tm = _row_tile(rows_per_batch)
    per = rows_per_batch // tm
    in_specs = [
        pl.BlockSpec((tm, d), lambda i: (i, 0)),
        _const_spec((1, d)),
        pl.BlockSpec((None, 6, d), lambda i: (mod_row(i // per), 0, 0)),
        _const_spec(w_qkv.shape),
    ]
    args = [x2d, nw, mods_l, w_qkv]
    if rope_tabs is not None:
        in_specs += [pl.BlockSpec((tm, LANES), lambda i: (i % per, 0))] * 2
        args += list(rope_tabs)
    return pl.pallas_call(
        functools.partial(_da_proj_kernel, rope=rope_tabs is not None, qk_width=qk_width, v_width=v_width),
        grid=(r // tm,),
        in_specs=in_specs,
        out_specs=[pl.BlockSpec((tm, qk_width), lambda i: (i, 0)),
                   pl.BlockSpec((tm, qk_width), lambda i: (i, 0)),
                   pl.BlockSpec((tm, v_width), lambda i: (i, 0))],
        out_shape=[jax.ShapeDtypeStruct((r, qk_width), BF16),
                   jax.ShapeDtypeStruct((r, qk_width), BF16),
                   jax.ShapeDtypeStruct((r, v_width), BF16)],
        compiler_params=_cparams(("arbitrary",)),
        name="da_proj",
    )(*args)


def _rope_tables(n_tokens):
    rows = n_tokens // GRID_W
    row = jnp.repeat(jnp.arange(rows, dtype=jnp.int32), GRID_W).astype(F32)
    col = jnp.tile(jnp.arange(GRID_W, dtype=jnp.int32), rows).astype(F32)
    n_freq = DA_HEAD_DIM // 4
    inv = ROPE_THETA ** (-jnp.arange(n_freq, dtype=F32) / n_freq)
    ang = jnp.concatenate([row[:, None] * inv, col[:, None] * inv], axis=-1)
    cos, sin = jnp.cos(ang), jnp.sin(ang)
    reps = LANES // DA_HEAD_DIM
    return (jnp.tile(jnp.concatenate([cos, cos], axis=-1), (1, reps)),
            jnp.tile(jnp.concatenate([-sin, sin], axis=-1), (1, reps)))


def _flash_kernel(*refs, n_seg, seg_lens, tk, sub_rows, lambda_init):
    q_ref = refs[0]
    k_refs = refs[1:1 + n_seg]
    v_refs = refs[1 + n_seg:1 + 2 * n_seg]
    lam_ref, subln_ref, y_ref, s_ref, mx_ref, sum_ref = refs[1 + 2 * n_seg:]
    tq = q_ref.shape[0]
    lam = lam_ref[...]
    lam_full = (jnp.exp(jnp.sum(lam[0:1, :] * lam[1:2, :], axis=-1, keepdims=True))
                - jnp.exp(jnp.sum(lam[2:3, :] * lam[3:4, :], axis=-1, keepdims=True)) + lambda_init)
    lane = lax.broadcasted_iota(jnp.int32, (1, LANES), 1)
    map_masks = (lane < DA_HEAD_DIM, lane >= DA_HEAD_DIM)
    tiles = [(sg, r) for sg in range(n_seg) for r in range(0, seg_lens[sg], tk)]
    halves = [slice(h * LANES, (h + 1) * LANES) for h in range(tk // LANES)]
    sub = min(sub_rows, tq)

    def pass_a(rows):
        q = q_ref[rows, :]
        for mp in range(2):
            qz = jnp.where(map_masks[mp], q, jnp.zeros_like(q))
            run_max = None
            for t, (sg, r) in enumerate(tiles):
                k = k_refs[sg][r:r + tk, :]
                s = lax.dot_general(qz, k, (((1,), (1,)), ((), ())), preferred_element_type=F32)
                s_ref[mp, t, rows, :] = s
                for hs in halves:
                    run_max = s[:, hs] if run_max is None else jnp.maximum(run_max, s[:, hs])
            mx_ref[mp, rows, :] = jnp.broadcast_to(jnp.max(run_max, axis=-1, keepdims=True), (sub, LANES))

    def pass_b(rows):
        rb = 64
        for mp in range(2):
            for r0 in range(rows.start, rows.stop, rb):
                blk = slice(r0, r0 + rb)
                row_max = jnp.concatenate([mx_ref[mp, blk, :]] * len(halves), axis=1)
                run_sum = None
                for t in range(len(tiles)):
                    e = jnp.exp2(s_ref[mp, t, blk, :] - row_max)
                    s_ref[mp, t, blk, :] = e
                    for hs in halves:
                        run_sum = e[:, hs] if run_sum is None else run_sum + e[:, hs]
                total = jnp.sum(run_sum, axis=-1, keepdims=True)
                coef = (1.0 / total) if mp == 0 else (lam_full / total)
                sum_ref[mp, blk, :] = jnp.broadcast_to(coef, (rb, LANES))

    def pass_c(rows):
        c0 = jnp.concatenate([sum_ref[0, rows, :]] * len(halves), axis=1)
        c1 = jnp.concatenate([sum_ref[1, rows, :]] * len(halves), axis=1)
        acc = None
        for t, (sg, r) in enumerate(tiles):
            p = (s_ref[0, t, rows, :] * c0 - s_ref[1, t, rows, :] * c1).astype(BF16)
            part = _dot(p, v_refs[sg][r:r + tk, :])
            acc = part if acc is None else acc + part
        y = acc * lax.rsqrt(jnp.mean(acc * acc, axis=-1, keepdims=True) + NORM_EPS) * subln_ref[...]
        y_ref[rows, :] = (y * (1.0 - lambda_init)).astype(BF16)

    subs = [slice(r0, r0 + sub) for r0 in range(0, tq, sub)]
    pass_a(subs[0])
    for u in range(1, len(subs)):
        pass_a(subs[u])
        pass_b(subs[u - 1])
        pass_c(subs[u - 1])
    pass_b(subs[-1])
    pass_c(subs[-1])


def _flash(q, ks, vs, seg_lens, q_rows_per_batch, lam, subln, lambda_init, batch, heads):
    tq = min(1024, q_rows_per_batch)
    tk = 256
    nq = q_rows_per_batch // tq
    n_seg = len(ks)
    n_tiles = sum(n // tk for n in seg_lens)
    assert all(n % tk == 0 for n in seg_lens)
    in_specs = [pl.BlockSpec((tq, LANES), lambda b, h, i: (b * nq + i, h))]
    for n in list(seg_lens) * 2:
        in_specs.append(pl.BlockSpec((n, LANES), lambda b, h, i: (b, h)))
    in_specs += [pl.BlockSpec(lam.shape, lambda b, h, i: (0, 0)), pl.BlockSpec((1, LANES), lambda b, h, i: (0, 0))]
    return pl.pallas_call(
        functools.partial(_flash_kernel, n_seg=n_seg, seg_lens=tuple(seg_lens), tk=tk, sub_rows=256,
                          lambda_init=lambda_init),
        grid=(batch, heads, nq),
        in_specs=in_specs,
        out_specs=pl.BlockSpec((tq, LANES), lambda b, h, i: (b * nq + i, h)),
        out_shape=jax.ShapeDtypeStruct((q.shape[0], heads * DA_V_DIM), BF16),
        scratch_shapes=[pltpu.VMEM((2, n_tiles, tq, tk), F32),
                        pltpu.VMEM((2, tq, LANES), F32),
                        pltpu.VMEM((2, tq, LANES), F32)],
        compiler_params=_cparams(("arbitrary", "arbitrary", "arbitrary")),
        name="diff_flash",
    )(q, *ks, *vs, lam, subln)


def _post_kernel(*refs, final, ff_tile):
    if final:
        x_ref, y_ref, mod_ref, wo_ref, nw_ref, w1_ref, w2_ref, fw_ref, o_ref, acc_ref = refs
    else:
        x_ref, y_ref, mod_ref, wo_ref, nw_ref, w1_ref, w2_ref, o_ref, acc_ref = refs
    mod = mod_ref[...]
    x1 = x_ref[...] + mod[2:3, :] * _dot(y_ref[...], wo_ref[...])
    h = _norm_mod(x1, nw_ref[...], mod, 3, 4).astype(BF16)
    d_ff = w1_ref.shape[1]
    for f in range(0, d_ff, ff_tile):
        a = jnp.maximum(_dot(h, w1_ref[:, f:f + ff_tile]), 0.0)
        part = _dot((a * a).astype(BF16), w2_ref[f:f + ff_tile, :])
        if f == 0:
            acc_ref[...] = part
        else:
            acc_ref[...] += part
    x2 = x1 + mod[5:6, :] * acc_ref[...]
    if final:
        x2 = x2 * lax.rsqrt(jnp.mean(x2 * x2, axis=-1, keepdims=True) + NORM_EPS) * fw_ref[...]
    o_ref[...] = x2


def _post(x2d, y2d, rows_per_batch, mod_row, mods_l, w_out, nw, w1, w2, final_w):
    r, d = x2d.shape
    tm = _row_tile(rows_per_batch)
    per = rows_per_batch // tm
    final = final_w is not None
    in_specs = [
        pl.BlockSpec((tm, d), lambda i: (i, 0)),
        pl.BlockSpec((tm, y2d.shape[1]), lambda i: (i, 0)),
        pl.BlockSpec((None, 6, d), lambda i: (mod_row(i // per), 0, 0)),
        _const_spec(w_out.shape),
        _const_spec((1, d)),
        _const_spec(w1.shape),
        _const_spec(w2.shape),
    ]
    args = [x2d, y2d, mods_l, w_out, nw, w1, w2]
    if final:
        in_specs.append(_const_spec((1, d)))
        args.append(final_w)
    return pl.pallas_call(
        functools.partial(_post_kernel, final=final, ff_tile=1024),
        grid=(r // tm,),
        in_specs=in_specs,
        out_specs=pl.BlockSpec((tm, d), lambda i: (i, 0)),
        out_shape=jax.ShapeDtypeStruct((r, d), F32),
        scratch_shapes=[pltpu.VMEM((tm, d), F32)],
        compiler_params=_cparams(("arbitrary",)),
        name="post_mlp",
    )(*args)


def _dn_gate_layout(n_heads, hb):
    ng = n_heads // hb
    idx = np.full((ng, LANES), -1, np.int64)
    for g in range(ng):
        for kind in range(4):
            for j in range(hb):
                idx[g, kind * hb + j] = kind * n_heads + g * hb + j
    return idx.reshape(-1)


def kernel(x, c, ctx, c_ctx, ada_w, ada_b, norm_w, mlp_w1, mlp_w2, dn_w_in, dn_conv, dn_a_log, dn_dt_bias,
           dn_out_norm, dn_w_out, da_w_qkv, da_lambda, da_subln, da_w_out, final_norm):
    batch, lat_len, d = x.shape
    ctx_len = ctx.shape[1]
    depth = ada_w.shape[0]
    dn_width = dn_w_out.shape[1]
    dn_heads = dn_width // DN_HEAD_DIM
    da_v_width = da_w_out.shape[1]
    da_heads = da_v_width // DA_V_DIM
    da_qk_width = da_heads * 2 * DA_HEAD_DIM
    hb = DN_HEADS_PER_STEP

    n_rows = -(-(batch + 1) // SUBLANES) * SUBLANES
    cc = jnp.zeros((n_rows, d), F32).at[:batch].set(c).at[batch].set(c_ctx)
    mods = _modulation(cc, ada_w, ada_b).reshape(depth, n_rows, 6, d)
    lat_row = lambda b: b
    ctx_row = lambda b: batch

    xl = x.reshape(batch * lat_len, d)
    xc = ctx.reshape(batch * ctx_len, d)

    gate_idx = _dn_gate_layout(dn_heads, hb)
    gate_valid = jnp.asarray(gate_idx >= 0)
    gate_src = jnp.asarray(np.maximum(gate_idx, 0))
    ng = dn_heads // hb
    rope_tabs = _rope_tables(lat_len)

    for i in range(depth):
        last = i == depth - 1
        mods_l = mods[i]
        j = i // 2
        if i % 2 == 0:
            w_in = dn_w_in[j]
            w_main = w_in[:, :4 * dn_width].astype(BF16)
            w_gate = jnp.where(gate_valid[None, :], jnp.take(w_in[:, 4 * dn_width:], gate_src, axis=1), 0.0).astype(BF16)
            conv_w = jnp.zeros((SUBLANES, 3 * dn_width), F32).at[:DN_CONV_W].set(dn_conv[j])
            def gate_rows(p):
                rows = jnp.transpose(p.reshape(2, ng, hb), (1, 0, 2)).reshape(ng, 2 * hb, 1)
                return jnp.zeros((ng, SUBLANES, LANES), F32).at[:, :2 * hb, :].set(
                    jnp.broadcast_to(rows, (ng, 2 * hb, LANES)))

            alog_g, dtb_g = gate_rows(dn_a_log[j]), gate_rows(dn_dt_bias[j])
            nw0 = norm_w[i, 0].reshape(1, d)
            qkv_l, kt_l, z_l, g_l = _dn_proj(xl, lat_len, lat_row, nw0, mods_l, w_main, w_gate, conv_w)
            qkv_c, kt_c, z_c, g_c = _dn_proj(xc, ctx_len, ctx_row, nw0, mods_l, w_main, w_gate, conv_w)
            y_c, y_l = _dn_core(qkv_c, qkv_l, kt_c, kt_l, z_c, z_l, g_c, g_l, alog_g, dtb_g,
                                dn_out_norm[j].reshape(1, DN_HEAD_DIM), batch, ctx_len, lat_len)
            w_out = dn_w_out[j].astype(BF16)
        else:
            lambda_init = 0.8 - 0.6 * math.exp(-0.3 * i)
            w_qkv = da_w_qkv[j].astype(BF16)
            nw0 = norm_w[i, 0].reshape(1, d)
            q_l, k_l, v_l = _da_proj(xl, lat_len, lat_row, nw0, mods_l, w_qkv, rope_tabs, da_qk_width, da_v_width)
            q_c, k_c, v_c = _da_proj(xc, ctx_len, ctx_row, nw0, mods_l, w_qkv, None, da_qk_width, da_v_width)
            lam = da_lambda[j]
            subln = da_subln[j].reshape(1, DA_V_DIM)
            y_l = _flash(q_l, (k_c, k_l), (v_c, v_l), (ctx_len, lat_len), lat_len, lam, subln, lambda_init,
                         batch, da_heads)
            y_c = None if last else _flash(q_c, (k_c,), (v_c,), (ctx_len,), ctx_len, lam, subln, lambda_init,
                                           batch, da_heads)
            w_out = da_w_out[j].astype(BF16)
        nw1 = norm_w[i, 1].reshape(1, d)
        w1 = mlp_w1[i].astype(BF16)
        w2 = mlp_w2[i].astype(BF16)
        xl = _post(xl, y_l, lat_len, lat_row, mods_l, w_out, nw1, w1, w2,
                   final_norm.reshape(1, d) if last else None)
        if not last:
            xc = _post(xc, y_c, ctx_len, ctx_row, mods_l, w_out, nw1, w1, w2, None)
    return xl.reshape(batch, lat_len, d)
```

```python
import functools
import math

import jax
import jax.numpy as jnp
import numpy as np
from jax import lax
from jax.experimental import pallas as pl
from jax.experimental.pallas import tpu as pltpu

F32 = jnp.float32
BF16 = jnp.bfloat16

NORM_EPS = 1e-6
ROPE_THETA = 10000.0
GRID_W = 64
LANES = 128
SUBLANES = 8
VMEM_LIMIT = 56 * 1024 * 1024

DN_HEAD_DIM = 128
DN_CONV_W = 5
DN_CHUNK = 128
DN_BASE = 16
DN_HEADS_PER_STEP = 2
DN_LOCAL_CHUNKS_PER_ITER = 3
DA_HEAD_DIM = 64
DA_V_DIM = 128
LOG2_E = 1.4426950408889634


def _dot(a, b):
    return jnp.dot(a, b, preferred_element_type=F32)


def _mm(a, b):
    return jnp.dot(a.astype(BF16), b.astype(BF16), preferred_element_type=F32)


def _silu(x):
    return x * jax.nn.sigmoid(x)


def _cparams(sem):
    return pltpu.CompilerParams(dimension_semantics=sem, vmem_limit_bytes=VMEM_LIMIT)


def _const_spec(shape):
    nd = len(shape)
    return pl.BlockSpec(shape, lambda *_: (0,) * nd, pipeline_mode=pl.Buffered(1))


def _mod_kernel(c_ref, w_ref, b_ref, o_ref):
    s = _silu(c_ref[...]).astype(BF16)
    o_ref[...] = _dot(s, w_ref[...].astype(BF16)) + b_ref[...]


def _modulation(cc, ada_w, ada_b):
    depth, d, n6 = ada_w.shape
    r = cc.shape[0]
    tn = n6 // 4
    return pl.pallas_call(
        _mod_kernel,
        grid=(depth, n6 // tn),
        in_specs=[
            pl.BlockSpec((r, d), lambda l, j: (0, 0)),
            pl.BlockSpec((None, d, tn), lambda l, j: (l, 0, j)),
            pl.BlockSpec((None, 1, tn), lambda l, j: (l, 0, j)),
        ],
        out_specs=pl.BlockSpec((None, r, tn), lambda l, j: (l, 0, j)),
        out_shape=jax.ShapeDtypeStruct((depth, r, n6), F32),
        compiler_params=_cparams(("arbitrary", "arbitrary")),
        name="adaln_mod",
    )(cc, ada_w, ada_b.reshape(depth, 1, n6))


def _norm_mod(x, nw, mod, shift_idx, scale_idx):
    y = x * lax.rsqrt(jnp.mean(x * x, axis=-1, keepdims=True) + NORM_EPS) * nw
    return y * (1.0 + mod[scale_idx:scale_idx + 1, :]) + mod[shift_idx:shift_idx + 1, :]


def _row_tile(rows_per_batch):
    tm = min(512, rows_per_batch)
    assert rows_per_batch % tm == 0
    return tm


DN_PROJ_HALO = 16


def _dn_proj_kernel(x_ref, xt_ref, xb_ref, nw_ref, mod_ref, wm_ref, wg_ref, cw_ref,
                    qkv_ref, kt_ref, z_ref, g_ref, *, width, per):
    halo = DN_PROJ_HALO
    tm = x_ref.shape[0]
    pos = pl.program_id(0) % per
    first, last = pos == 0, pos == per - 1
    xx = jnp.concatenate([xt_ref[...], x_ref[...], xb_ref[...]], axis=0)
    hh = _norm_mod(xx, nw_ref[...], mod_ref[...], 0, 1).astype(BF16)
    h = hh[halo:halo + tm, :]
    tn = 512
    for j0 in range(0, 2 * width, tn):
        wide = _dot(hh, wm_ref[:, j0:j0 + tn])
        wide = jnp.concatenate([jnp.where(first, 0.0, wide[0:halo, :]), wide[halo:halo + tm, :],
                                jnp.where(last, 0.0, wide[halo + tm:, :])], axis=0)
        for j in range(j0, j0 + tn, LANES):
            win = wide[:, j - j0:j - j0 + LANES]
            acc = None
            for d in range(DN_CONV_W):
                shift = d - DN_CONV_W // 2
                moved = win if shift == 0 else pltpu.roll(win, (-shift) % win.shape[0], 0)
                term = cw_ref[d:d + 1, j:j + LANES] * moved[halo:halo + tm, :]
                acc = term if acc is None else acc + term
            y = _silu(acc)
            y = y * lax.rsqrt(jnp.sum(y * y, axis=-1, keepdims=True) + 1e-6)
            if j < width:
                y = y * (DN_HEAD_DIM ** -0.5)
            qkv_ref[:, j:j + LANES] = y
            if j >= width:
                kt_ref[j - width:j - width + LANES, :] = y.T
    for j in range(2 * width, 3 * width, tn):
        qkv_ref[:, j:j + tn] = _dot(h, wm_ref[:, j:j + tn])
    for j in range(0, width, tn):
        z_ref[:, j:j + tn] = _dot(h, wm_ref[:, 3 * width + j:3 * width + j + tn])
    g_ref[...] = _dot(h, wg_ref[...])


def _dn_proj(x2d, rows_per_batch, mod_row, nw, mods_l, w_main, w_gate, conv_w):
    r, d = x2d.shape
    width = w_main.shape[1] // 4
    gw = w_gate.shape[1]
    tm = _row_tile(rows_per_batch)
    per = rows_per_batch // tm
    halo = DN_PROJ_HALO
    hpt = tm // halo
    return pl.pallas_call(
        functools.partial(_dn_proj_kernel, width=width, per=per),
        grid=(r // tm,),
        in_specs=[
            pl.BlockSpec((tm, d), lambda i: (i, 0)),
            pl.BlockSpec((halo, d), lambda i: (jnp.maximum(i * hpt - 1, 0), 0)),
            pl.BlockSpec((halo, d), lambda i: (jnp.minimum((i + 1) * hpt, r // halo - 1), 0)),
            _const_spec((1, d)),
            pl.BlockSpec((None, 6, d), lambda i: (mod_row(i // per), 0, 0)),
            _const_spec(w_main.shape),
            _const_spec(w_gate.shape),
            _const_spec(conv_w.shape),
        ],
        out_specs=[
            pl.BlockSpec((tm, 3 * width), lambda i: (i, 0)),
            pl.BlockSpec((width, tm), lambda i: (0, i)),
            pl.BlockSpec((tm, width), lambda i: (i, 0)),
            pl.BlockSpec((tm, gw), lambda i: (i, 0)),
        ],
        out_shape=[
            jax.ShapeDtypeStruct((r, 3 * width), F32),
            jax.ShapeDtypeStruct((width, r), F32),
            jax.ShapeDtypeStruct((r, width), F32),
            jax.ShapeDtypeStruct((r, gw), F32),
        ],
        compiler_params=_cparams(("arbitrary",)),
        name="dn_proj",
    )(x2d, x2d, x2d, nw, mods_l, w_main, w_gate, conv_w)


def _cumsum_lanes(x, reverse):
    n = x.shape[1]
    lane = lax.broadcasted_iota(jnp.int32, x.shape, 1)
    s = 1
    while s < n:
        if reverse:
            x = x + jnp.where(lane < n - s, pltpu.roll(x, n - s, 1), 0.0)
        else:
            x = x + jnp.where(lane >= s, pltpu.roll(x, s, 1), 0.0)
        s *= 2
    return x


def _unit_tri_inverse(mats):
    n = mats[0].shape[0]
    row = lax.broadcasted_iota(jnp.int32, (n, n), 0)
    col = lax.broadcasted_iota(jnp.int32, (n, n), 1)

    def same_block(size):
        return (row // size) == (col // size)

    base = same_block(DN_BASE)
    negs = [-a for a in mats]
    accs = [jnp.where(base, na, 0.0) for na in negs]
    steps = int(math.log2(DN_BASE))
    ps = [_mm(p, p) for p in accs]
    for k in range(1, steps):
        if k + 1 < steps:
            both = [_mm(p, jnp.concatenate([acc, p], axis=1)) for acc, p in zip(accs, ps)]
            accs = [acc + p + b[:, :n] for acc, p, b in zip(accs, ps, both)]
            ps = [b[:, n:] for b in both]
        else:
            accs = [acc + p + _mm(p, acc) for acc, p in zip(accs, ps)]
    xs = [jnp.where(row == col, 1.0, acc).astype(BF16) for acc in accs]
    size = DN_BASE
    while size < n:
        sel = same_block(2 * size) & jnp.logical_not(same_block(size))
        offs = [jnp.where(sel, na, 0.0).astype(BF16) for na in negs]
        ys = [_dot(x, off).astype(BF16) for x, off in zip(xs, offs)]
        xs = [x + _dot(y, x).astype(BF16) for x, y in zip(xs, ys)]
        size *= 2
    return xs


def _dn_core_kernel(qc_ref, kc_ref, vc_ref, ql_ref, kl_ref, vl_ref, ktc_ref, ktl_ref, zc_ref, zl_ref, gc_ref, gl_ref,
                    cwv_ref, alog_ref, dtb_ref, onorm_ref,
                    yc_ref, yl_ref,
                    gb_ref, gbt_ref, u_ref, w_ref, qkm_ref, o_ref, st_ref, *, hb, ctx_len, lat_len):
    C = DN_CHUNK
    seq = ctx_len + lat_len
    n_ctx, n_chunks = ctx_len // C, seq // C
    width = hb * LANES

    assert 4 * hb <= SUBLANES
    raws = []
    for c in range(n_chunks):
        src = gc_ref[c * C:(c + 1) * C, :] if c < n_ctx else gl_ref[(c - n_ctx) * C:(c - n_ctx + 1) * C, :]
        raws.append(src.T[0:SUBLANES, :])
    raw = jnp.concatenate(raws, axis=0)
    grow_id = lax.broadcasted_iota(jnp.int32, raw.shape, 0) % SUBLANES
    a = raw + jnp.concatenate([dtb_ref[...]] * n_chunks, axis=0)
    g = (jnp.concatenate([-jnp.exp(alog_ref[...])] * n_chunks, axis=0)
         * (jnp.maximum(a, 0.0) + jnp.log1p(jnp.exp(-jnp.abs(a)))))
    pre = _cumsum_lanes(g, reverse=False)
    suf = _cumsum_lanes(g, reverse=True)
    vals = jnp.where(grow_id < hb, pre, jnp.where(grow_id < 2 * hb, suf, jax.nn.sigmoid(raw)))
    zero_rows = jnp.zeros((C - SUBLANES, C), F32)
    for c in range(n_chunks):
        tile = vals[c * SUBLANES:(c + 1) * SUBLANES, :]
        gbt_ref[:, c * C:(c + 1) * C] = tile
        gb_ref[c * C:(c + 1) * C, :] = jnp.concatenate([tile, zero_rows], axis=0).T

    def source(c):
        if isinstance(c, int) and c < n_ctx:
            return qc_ref, kc_ref, vc_ref, ktc_ref, c * C
        r = (c - n_ctx) * C
        return ql_ref, kl_ref, vl_ref, ktl_ref, (r if isinstance(c, int) else pl.multiple_of(r, C))

    def scratch_row(c):
        return c * C if isinstance(c, int) else pl.multiple_of(c * C, C)

    row = lax.broadcasted_iota(jnp.int32, (C, C), 0)
    col = lax.broadcasted_iota(jnp.int32, (C, C), 1)

    halo = SUBLANES
    zeros_halo = jnp.zeros((halo, LANES), F32)

    def conv_v(c, v_ref, sr, cols):
        if isinstance(c, int):
            seg_len = ctx_len if c < n_ctx else lat_len
            top = zeros_halo if sr == 0 else v_ref[sr - halo:sr, cols]
            bot = zeros_halo if sr + C == seg_len else v_ref[sr + C:sr + C + halo, cols]
        else:
            top_start = pl.multiple_of(jnp.maximum(sr - halo, 0), halo)
            bot_start = pl.multiple_of(jnp.minimum(sr + C, lat_len - halo), halo)
            top = jnp.where(sr == 0, 0.0, v_ref[pl.ds(top_start, halo), cols])
            bot = jnp.where(sr + C == lat_len, 0.0, v_ref[pl.ds(bot_start, halo), cols])
        win = jnp.concatenate([top, v_ref[pl.ds(sr, C), cols], bot], axis=0)
        acc = None
        for d in range(DN_CONV_W):
            shift = d - DN_CONV_W // 2
            moved = win if shift == 0 else pltpu.roll(win, (-shift) % win.shape[0], 0)
            term = cwv_ref[d:d + 1, cols] * moved[halo:halo + C, :]
            acc = term if acc is None else acc + term
        return _silu(acc)

    def local_chunks(chunk_ids):
        chains, a_mats, rhss = [], [], []
        for c in chunk_ids:
            q_ref, k_ref, v_ref, kt_ref, sr = source(c)
            r = scratch_row(c)
            gb = gb_ref[pl.ds(r, C), :]
            gbt = gbt_ref[:, pl.ds(r, C)]
            for j in range(hb):
                cols = slice(j * LANES, (j + 1) * LANES)
                qn = q_ref[pl.ds(sr, C), cols]
                kn = k_ref[pl.ds(sr, C), cols]
                vn = conv_v(c, v_ref, sr, cols)
                knt = kt_ref[cols, pl.ds(sr, C)].astype(BF16)
                kk = _dot(kn.astype(BF16), knt)
                qk = _dot(qn.astype(BF16), knt)
                for dr in range(2):
                    gl_, bl_ = dr * hb + j, 2 * hb + dr * hb + j
                    gcol, grow, bcol = gb[:, gl_:gl_ + 1], gbt[gl_:gl_ + 1, :], gb[:, bl_:bl_ + 1]
                    incl = (row >= col) if dr == 0 else (row <= col)
                    strict = (row > col) if dr == 0 else (row < col)
                    dec = jnp.where(incl, jnp.exp(jnp.where(incl, gcol - grow, 0.0)), 0.0)
                    qkm_ref[dr, pl.ds(r, C), cols] = (qk * dec).astype(BF16)
                    chains.append((dr, r, cols))
                    a_mats.append(jnp.where(strict, kk * dec * bcol, 0.0))
                    rhss.append(jnp.concatenate([vn * bcol, kn * (bcol * jnp.exp(gcol))], axis=1))
        tinvs = _unit_tri_inverse(a_mats)
        uws = [_dot(tinv, rhs.astype(BF16)) for tinv, rhs in zip(tinvs, rhss)]
        for (dr, r, cols), uw in zip(chains, uws):
            u_ref[dr, pl.ds(r, C), cols] = uw[:, :LANES]
            w_ref[dr, pl.ds(r, C), cols] = uw[:, LANES:].astype(BF16)

    per_iter = max(p for p in range(1, DN_LOCAL_CHUNKS_PER_ITER + 1) if n_chunks % p == 0)
    n_groups = n_chunks // per_iter
    assert n_ctx <= per_iter
    local_chunks(list(range(per_iter)))

    def local_body(it, carry):
        local_chunks([it * per_iter + sub for sub in range(per_iter)])
        return carry

    lax.fori_loop(1, n_groups, local_body, 0)

    st_ref[...] = jnp.zeros(st_ref.shape, F32)
    o_ref[...] = jnp.zeros((seq, width), F32)

    def scan_step(s):
        chains = []
        for dr in range(2):
            if dr == 0:
                c = s
            elif isinstance(s, int):
                c = n_ctx - 1 - s if s < n_ctx else n_chunks - 1 - s + n_ctx
            else:
                c = n_chunks - 1 - s + n_ctx
            q_ref, _, _, kt_ref, sr = source(c)
            r = scratch_row(c)
            gb = gb_ref[pl.ds(r, C), :]
            gbt = gbt_ref[:, pl.ds(r, C)]
            for j in range(hb):
                cols = slice(j * LANES, (j + 1) * LANES)
                gl_ = dr * hb + j
                gcol, grow = gb[:, gl_:gl_ + 1], gbt[gl_:gl_ + 1, :]
                g_end_c = gcol[C - 1:C, :] if dr == 0 else gcol[0:1, :]
                g_end_r = grow[:, C - 1:C] if dr == 0 else grow[:, 0:1]
                qd = (q_ref[pl.ds(sr, C), cols] * jnp.exp(gcol)).astype(BF16)
                kdt = (kt_ref[cols, pl.ds(sr, C)] * jnp.exp(g_end_r - grow)).astype(BF16)
                chains.append((dr, j, r, cols, qd, kdt, jnp.exp(g_end_c)))
        states = [st_ref[dr * hb + j] for dr, j, *_ in chains]
        pss = [_dot(jnp.concatenate([w_ref[dr, pl.ds(r, C), cols], qd], axis=0), st.astype(BF16))
               for (dr, j, r, cols, qd, kdt, gt), st in zip(chains, states)]
        nus = [(u_ref[dr, pl.ds(r, C), cols] - ps[:C]).astype(BF16)
               for (dr, j, r, cols, qd, kdt, gt), ps in zip(chains, pss)]
        upds = [_dot(kdt, nu) for (dr, j, r, cols, qd, kdt, gt), nu in zip(chains, nus)]
        outs = [ps[C:] + _dot(qkm_ref[dr, pl.ds(r, C), cols], nu)
                for (dr, j, r, cols, qd, kdt, gt), ps, nu in zip(chains, pss, nus)]
        for (dr, j, r, cols, qd, kdt, gt), st, upd, out in zip(chains, states, upds, outs):
            st_ref[dr * hb + j] = st * gt + upd
            o_ref[pl.ds(r, C), cols] += out

    for s in range(n_ctx):
        scan_step(s)

    def scan_body(s, carry):
        scan_step(s)
        return carry

    lax.fori_loop(n_ctx, n_chunks, scan_body, 0)

    onorm = onorm_ref[...]
    tr = 256
    for r0 in range(0, seq, tr):
        for j in range(hb):
            cols = slice(j * LANES, (j + 1) * LANES)
            o = o_ref[r0:r0 + tr, cols]
            y = o * lax.rsqrt(jnp.mean(o * o, axis=-1, keepdims=True) + NORM_EPS) * onorm
            if r0 < ctx_len:
                yc_ref[r0:r0 + tr, cols] = (y * _silu(zc_ref[r0:r0 + tr, cols])).astype(BF16)
            else:
                q0 = r0 - ctx_len
                yl_ref[q0:q0 + tr, cols] = (y * _silu(zl_ref[q0:q0 + tr, cols])).astype(BF16)


def _dn_core(qkv_c, qkv_l, kt_c, kt_l, z_c, z_l, g_c, g_l, conv_w, alog_g, dtb_g, onorm, batch, ctx_len, lat_len):
    hb = DN_HEADS_PER_STEP
    width = hb * LANES
    dn_width = z_c.shape[1]
    ng = dn_width // width
    seq = ctx_len + lat_len
    assert ctx_len % 256 == 0 and lat_len % 256 == 0 and ctx_len % DN_CHUNK == 0

    def col_spec(rows, off):
        return pl.BlockSpec((rows, width), lambda b, g: (b, off + g))

    in_specs = [
        col_spec(ctx_len, 0), col_spec(ctx_len, ng), col_spec(ctx_len, 2 * ng),
        col_spec(lat_len, 0), col_spec(lat_len, ng), col_spec(lat_len, 2 * ng),
        pl.BlockSpec((width, ctx_len), lambda b, g: (g, b)),
        pl.BlockSpec((width, lat_len), lambda b, g: (g, b)),
        col_spec(ctx_len, 0), col_spec(lat_len, 0),
        pl.BlockSpec((ctx_len, LANES), lambda b, g: (b, g)),
        pl.BlockSpec((lat_len, LANES), lambda b, g: (b, g)),
        pl.BlockSpec((SUBLANES, width), lambda b, g: (0, 2 * ng + g)),
        pl.BlockSpec((None, SUBLANES, LANES), lambda b, g: (g, 0, 0)),
        pl.BlockSpec((None, SUBLANES, LANES), lambda b, g: (g, 0, 0)),
        pl.BlockSpec((1, LANES), lambda b, g: (0, 0)),
    ]
    scratch = [
        pltpu.VMEM((seq, LANES), F32),
        pltpu.VMEM((SUBLANES, seq), F32),
        pltpu.VMEM((2, seq, width), F32),
        pltpu.VMEM((2, seq, width), BF16),
        pltpu.VMEM((2, seq, width), BF16),
        pltpu.VMEM((seq, width), F32),
        pltpu.VMEM((2 * hb, DN_HEAD_DIM, DN_HEAD_DIM), F32),
    ]
    return pl.pallas_call(
        functools.partial(_dn_core_kernel, hb=hb, ctx_len=ctx_len, lat_len=lat_len),
        grid=(batch, ng),
        in_specs=in_specs,
        out_specs=[col_spec(ctx_len, 0), col_spec(lat_len, 0)],
        out_shape=[jax.ShapeDtypeStruct((batch * ctx_len, dn_width), BF16),
                   jax.ShapeDtypeStruct((batch * lat_len, dn_width), BF16)],
        scratch_shapes=scratch,
        compiler_params=_cparams(("arbitrary", "arbitrary")),
        name="dn_core",
    )(qkv_c, qkv_c, qkv_c, qkv_l, qkv_l, qkv_l, kt_c, kt_l, z_c, z_l, g_c, g_l, conv_w, alog_g, dtb_g, onorm)


def _da_proj_kernel(*refs, rope, qk_width, v_width):
    if rope:
        x_ref, nw_ref, mod_ref, w_ref, cos_ref, sin_ref, q_ref, k_ref, v_ref = refs
    else:
        x_ref, nw_ref, mod_ref, w_ref, q_ref, k_ref, v_ref = refs
    h = _norm_mod(x_ref[...], nw_ref[...], mod_ref[...], 0, 1).astype(BF16)
    tm = h.shape[0]
    if rope:
        cos, sin = cos_ref[...], sin_ref[...]
        lane = lax.broadcasted_iota(jnp.int32, (tm, LANES), 1)
        first_half = (lane % DA_HEAD_DIM) < (DA_HEAD_DIM // 2)
    tn = 512
    for which, dst in ((0, q_ref), (1, k_ref)):
        for j0 in range(0, qk_width, tn):
            wide = _dot(h, w_ref[:, which * qk_width + j0:which * qk_width + j0 + tn])
            for j in range(0, tn, LANES):
                y = wide[:, j:j + LANES]
                if which == 0:
                    y = y * (DA_HEAD_DIM ** -0.5 * LOG2_E)
                if rope:
                    half = DA_HEAD_DIM // 2
                    partner = jnp.where(first_half, pltpu.roll(y, LANES - half, 1), pltpu.roll(y, half, 1))
                    y = y * cos + partner * sin
                dst[:, j0 + j:j0 + j + LANES] = y.astype(BF16)
    for j in range(0, v_width, 512):
        v_ref[:, j:j + 512] = _dot(h, w_ref[:, 2 * qk_width + j:2 * qk_width + j + 512]).astype(BF16)


def _da_proj(x2d, rows_per_batch, mod_row, nw, mods_l, w_qkv, rope_tabs, qk_width, v_width):
    r, d = x2d.shape
    tm = _row_tile(rows_per_batch)
    per = rows_per_batch // tm
    in_specs = [
        pl.BlockSpec((tm, d), lambda i: (i, 0)),
        _const_spec((1, d)),
        pl.BlockSpec((None, 6, d), lambda i: (mod_row(i // per), 0, 0)),
        _const_spec(w_qkv.shape),
    ]
    args = [x2d, nw, mods_l, w_qkv]
    if rope_tabs is not None:
        in_specs += [pl.BlockSpec((tm, LANES), lambda i: (i % per, 0))] * 2
        args += list(rope_tabs)
    return pl.pallas_call(
        functools.partial(_da_proj_kernel, rope=rope_tabs is not None, qk_width=qk_width, v_width=v_width),
        grid=(r // tm,),
        in_specs=in_specs,
        out_specs=[pl.BlockSpec((tm, qk_width), lambda i: (i, 0)),
                   pl.BlockSpec((tm, qk_width), lambda i: (i, 0)),
                   pl.BlockSpec((tm, v_width), lambda i: (i, 0))],
        out_shape=[jax.ShapeDtypeStruct((r, qk_width), BF16),
                   jax.ShapeDtypeStruct((r, qk_width), BF16),
                   jax.ShapeDtypeStruct((r, v_width), BF16)],
        compiler_params=_cparams(("arbitrary",)),
        name="da_proj",
    )(*args)


def _rope_tables(n_tokens):
    rows = n_tokens // GRID_W
    row = jnp.repeat(jnp.arange(rows, dtype=jnp.int32), GRID_W).astype(F32)
    col = jnp.tile(jnp.arange(GRID_W, dtype=jnp.int32), rows).astype(F32)
    n_freq = DA_HEAD_DIM // 4
    inv = ROPE_THETA ** (-jnp.arange(n_freq, dtype=F32) / n_freq)
    ang = jnp.concatenate([row[:, None] * inv, col[:, None] * inv], axis=-1)
    cos, sin = jnp.cos(ang), jnp.sin(ang)
    reps = LANES // DA_HEAD_DIM
    return (jnp.tile(jnp.concatenate([cos, cos], axis=-1), (1, reps)),
            jnp.tile(jnp.concatenate([-sin, sin], axis=-1), (1, reps)))


def _flash_kernel(*refs, n_seg, seg_lens, tk, sub_rows, lambda_init):
    q_ref = refs[0]
    k_refs = refs[1:1 + n_seg]
    v_refs = refs[1 + n_seg:1 + 2 * n_seg]
    lam_ref, subln_ref, y_ref, s_ref, mx_ref, sum_ref = refs[1 + 2 * n_seg:]
    tq = q_ref.shape[0]
    lam = lam_ref[...]
    lam_full = (jnp.exp(jnp.sum(lam[0:1, :] * lam[1:2, :], axis=-1, keepdims=True))
                - jnp.exp(jnp.sum(lam[2:3, :] * lam[3:4, :], axis=-1, keepdims=True)) + lambda_init)
    lane = lax.broadcasted_iota(jnp.int32, (1, LANES), 1)
    map_masks = (lane < DA_HEAD_DIM, lane >= DA_HEAD_DIM)
    tiles = [(sg, r) for sg in range(n_seg) for r in range(0, seg_lens[sg], tk)]
    halves = [slice(h * LANES, (h + 1) * LANES) for h in range(tk // LANES)]
    sub = min(sub_rows, tq)

    def pass_a(rows):
        q = q_ref[rows, :]
        for mp in range(2):
            qz = jnp.where(map_masks[mp], q, jnp.zeros_like(q))
            run_max = None
            for t, (sg, r) in enumerate(tiles):
                k = k_refs[sg][r:r + tk, :]
                s = lax.dot_general(qz, k, (((1,), (1,)), ((), ())), preferred_element_type=F32)
                s_ref[mp, t, rows, :] = s
                for hs in halves:
                    run_max = s[:, hs] if run_max is None else jnp.maximum(run_max, s[:, hs])
            mx_ref[mp, rows, :] = jnp.broadcast_to(jnp.max(run_max, axis=-1, keepdims=True), (sub, LANES))

    def pass_b(rows):
        rb = 64
        for mp in range(2):
            for r0 in range(rows.start, rows.stop, rb):
                blk = slice(r0, r0 + rb)
                row_max = jnp.concatenate([mx_ref[mp, blk, :]] * len(halves), axis=1)
                run_sum = None
                for t in range(len(tiles)):
                    e = jnp.exp2(s_ref[mp, t, blk, :] - row_max)
                    s_ref[mp, t, blk, :] = e
                    for hs in halves:
                        run_sum = e[:, hs] if run_sum is None else run_sum + e[:, hs]
                total = jnp.sum(run_sum, axis=-1, keepdims=True)
                coef = (1.0 / total) if mp == 0 else (lam_full / total)
                sum_ref[mp, blk, :] = jnp.broadcast_to(coef, (rb, LANES))

    def pass_c(rows):
        c0 = jnp.concatenate([sum_ref[0, rows, :]] * len(halves), axis=1)
        c1 = jnp.concatenate([sum_ref[1, rows, :]] * len(halves), axis=1)
        acc = None
        for t, (sg, r) in enumerate(tiles):
            p = (s_ref[0, t, rows, :] * c0 - s_ref[1, t, rows, :] * c1).astype(BF16)
            part = _dot(p, v_refs[sg][r:r + tk, :])
            acc = part if acc is None else acc + part
        y = acc * lax.rsqrt(jnp.mean(acc * acc, axis=-1, keepdims=True) + NORM_EPS) * subln_ref[...]
        y_ref[rows, :] = (y * (1.0 - lambda_init)).astype(BF16)

    subs = [slice(r0, r0 + sub) for r0 in range(0, tq, sub)]
    pass_a(subs[0])
    for u in range(1, len(subs)):
        pass_a(subs[u])
        pass_b(subs[u - 1])
        pass_c(subs[u - 1])
    pass_b(subs[-1])
    pass_c(subs[-1])


def _flash(q, ks, vs, seg_lens, q_rows_per_batch, lam, subln, lambda_init, batch, heads):
    tq = min(2048, q_rows_per_batch)
    tk = 256
    nq = q_rows_per_batch // tq
    n_seg = len(ks)
    n_tiles = sum(n // tk for n in seg_lens)
    assert all(n % tk == 0 for n in seg_lens)
    in_specs = [pl.BlockSpec((tq, LANES), lambda b, h, i: (b * nq + i, h))]
    for n in list(seg_lens) * 2:
        in_specs.append(pl.BlockSpec((n, LANES), lambda b, h, i: (b, h)))
    in_specs += [pl.BlockSpec(lam.shape, lambda b, h, i: (0, 0)), pl.BlockSpec((1, LANES), lambda b, h, i: (0, 0))]
    return pl.pallas_call(
        functools.partial(_flash_kernel, n_seg=n_seg, seg_lens=tuple(seg_lens), tk=tk, sub_rows=256,
                          lambda_init=lambda_init),
        grid=(batch, heads, nq),
        in_specs=in_specs,
        out_specs=pl.BlockSpec((tq, LANES), lambda b, h, i: (b * nq + i, h)),
        out_shape=jax.ShapeDtypeStruct((q.shape[0], heads * DA_V_DIM), BF16),
        scratch_shapes=[pltpu.VMEM((2, n_tiles, tq, tk), F32),
                        pltpu.VMEM((2, tq, LANES), F32),
                        pltpu.VMEM((2, tq, LANES), F32)],
        compiler_params=_cparams(("arbitrary", "arbitrary", "arbitrary")),
        name="diff_flash",
    )(q, *ks, *vs, lam, subln)


def _post_kernel(*refs, final, ff_tile):
    if final:
        x_ref, y_ref, mod_ref, wo_ref, nw_ref, w1_ref, w2_ref, fw_ref, o_ref, acc_ref = refs
    else:
        x_ref, y_ref, mod_ref, wo_ref, nw_ref, w1_ref, w2_ref, o_ref, acc_ref = refs
    mod = mod_ref[...]
    x1 = x_ref[...] + mod[2:3, :] * _dot(y_ref[...], wo_ref[...])
    h = _norm_mod(x1, nw_ref[...], mod, 3, 4).astype(BF16)
    d_ff = w1_ref.shape[1]
    for f in range(0, d_ff, ff_tile):
        a = jnp.maximum(_dot(h, w1_ref[:, f:f + ff_tile]), 0.0)
        part = _dot((a * a).astype(BF16), w2_ref[f:f + ff_tile, :])
        if f == 0:
            acc_ref[...] = part
        else:
            acc_ref[...] += part
    x2 = x1 + mod[5:6, :] * acc_ref[...]
    if final:
        x2 = x2 * lax.rsqrt(jnp.mean(x2 * x2, axis=-1, keepdims=True) + NORM_EPS) * fw_ref[...]
    o_ref[...] = x2


def _post(x2d, y2d, rows_per_batch, mod_row, mods_l, w_out, nw, w1, w2, final_w):
    r, d = x2d.shape
    tm = _row_tile(rows_per_batch)
    per = rows_per_batch // tm
    final = final_w is not None
    in_specs = [
        pl.BlockSpec((tm, d), lambda i: (i, 0)),
        pl.BlockSpec((tm, y2d.shape[1]), lambda i: (i, 0)),
        pl.BlockSpec((None, 6, d), lambda i: (mod_row(i // per), 0, 0)),
        _const_spec(w_out.shape),
        _const_spec((1, d)),
        _const_spec(w1.shape),
        _const_spec(w2.shape),
    ]
    args = [x2d, y2d, mods_l, w_out, nw, w1, w2]
    if final:
        in_specs.append(_const_spec((1, d)))
        args.append(final_w)
    return pl.pallas_call(
        functools.partial(_post_kernel, final=final, ff_tile=1024),
        grid=(r // tm,),
        in_specs=in_specs,
        out_specs=pl.BlockSpec((tm, d), lambda i: (i, 0)),
        out_shape=jax.ShapeDtypeStruct((r, d), F32),
        scratch_shapes=[pltpu.VMEM((tm, d), F32)],
        compiler_params=_cparams(("arbitrary",)),
        name="post_mlp",
    )(*args)


def _dn_gate_layout(n_heads, hb):
    ng = n_heads // hb
    idx = np.full((ng, LANES), -1, np.int64)
    for g in range(ng):
        for kind in range(4):
            for j in range(hb):
                idx[g, kind * hb + j] = kind * n_heads + g * hb + j
    return idx.reshape(-1)


def kernel(x, c, ctx, c_ctx, ada_w, ada_b, norm_w, mlp_w1, mlp_w2, dn_w_in, dn_conv, dn_a_log, dn_dt_bias,
           dn_out_norm, dn_w_out, da_w_qkv, da_lambda, da_subln, da_w_out, final_norm):
    batch, lat_len, d = x.shape
    ctx_len = ctx.shape[1]
    depth = ada_w.shape[0]
    dn_width = dn_w_out.shape[1]
    dn_heads = dn_width // DN_HEAD_DIM
    da_v_width = da_w_out.shape[1]
    da_heads = da_v_width // DA_V_DIM
    da_qk_width = da_heads * 2 * DA_HEAD_DIM
    hb = DN_HEADS_PER_STEP

    n_rows = -(-(batch + 1) // SUBLANES) * SUBLANES
    cc = jnp.zeros((n_rows, d), F32).at[:batch].set(c).at[batch].set(c_ctx)
    mods = _modulation(cc, ada_w, ada_b).reshape(depth, n_rows, 6, d)
    lat_row = lambda b: b
    ctx_row = lambda b: batch

    xl = x.reshape(batch * lat_len, d)
    xc = ctx.reshape(batch * ctx_len, d)

    gate_idx = _dn_gate_layout(dn_heads, hb)
    gate_valid = jnp.asarray(gate_idx >= 0)
    gate_src = jnp.asarray(np.maximum(gate_idx, 0))
    ng = dn_heads // hb
    rope_tabs = _rope_tables(lat_len)

    for i in range(depth):
        last = i == depth - 1
        mods_l = mods[i]
        j = i // 2
        if i % 2 == 0:
            w_in = dn_w_in[j]
            w_main = w_in[:, :4 * dn_width].astype(BF16)
            w_gate = jnp.where(gate_valid[None, :], jnp.take(w_in[:, 4 * dn_width:], gate_src, axis=1), 0.0).astype(BF16)
            conv_w = jnp.zeros((SUBLANES, 3 * dn_width), F32).at[:DN_CONV_W].set(dn_conv[j])
            def gate_rows(p):
                rows = jnp.transpose(p.reshape(2, ng, hb), (1, 0, 2)).reshape(ng, 2 * hb, 1)
                return jnp.zeros((ng, SUBLANES, LANES), F32).at[:, :2 * hb, :].set(
                    jnp.broadcast_to(rows, (ng, 2 * hb, LANES)))

            alog_g, dtb_g = gate_rows(dn_a_log[j]), gate_rows(dn_dt_bias[j])
            nw0 = norm_w[i, 0].reshape(1, d)
            qkv_l, kt_l, z_l, g_l = _dn_proj(xl, lat_len, lat_row, nw0, mods_l, w_main, w_gate, conv_w)
            qkv_c, kt_c, z_c, g_c = _dn_proj(xc, ctx_len, ctx_row, nw0, mods_l, w_main, w_gate, conv_w)
            y_c, y_l = _dn_core(qkv_c, qkv_l, kt_c, kt_l, z_c, z_l, g_c, g_l, conv_w, alog_g, dtb_g,
                                dn_out_norm[j].reshape(1, DN_HEAD_DIM), batch, ctx_len, lat_len)
            w_out = dn_w_out[j].astype(BF16)
        else:
            lambda_init = 0.8 - 0.6 * math.exp(-0.3 * i)
            w_qkv = da_w_qkv[j].astype(BF16)
            nw0 = norm_w[i, 0].reshape(1, d)
            q_l, k_l, v_l = _da_proj(xl, lat_len, lat_row, nw0, mods_l, w_qkv, rope_tabs, da_qk_width, da_v_width)
            q_c, k_c, v_c = _da_proj(xc, ctx_len, ctx_row, nw0, mods_l, w_qkv, None, da_qk_width, da_v_width)
            lam = da_lambda[j]
            subln = da_subln[j].reshape(1, DA_V_DIM)
            y_l = _flash(q_l, (k_c, k_l), (v_c, v_l), (ctx_len, lat_len), lat_len, lam, subln, lambda_init,
                         batch, da_heads)
            y_c = None if last else _flash(q_c, (k_c,), (v_c,), (ctx_len,), ctx_len, lam, subln, lambda_init,
                                           batch, da_heads)
            w_out = da_w_out[j].astype(BF16)
        nw1 = norm_w[i, 1].reshape(1, d)
        w1 = mlp_w1[i].astype(BF16)
        w2 = mlp_w2[i].astype(BF16)
        xl = _post(xl, y_l, lat_len, lat_row, mods_l, w_out, nw1, w1, w2,
                   final_norm.reshape(1, d) if last else None)
        if not last:
            xc = _post(xc, y_c, ctx_len, ctx_row, mods_l, w_out, nw1, w1, w2, None)
    return xl.reshape(batch, lat_len, d)
```

```python
import functools
import math

import jax
import jax.numpy as jnp
import numpy as np
from jax import lax
from jax.experimental import pallas as pl
from jax.experimental.pallas import tpu as pltpu

F32 = jnp.float32
BF16 = jnp.bfloat16

NORM_EPS = 1e-6
ROPE_THETA = 10000.0
GRID_W = 64
LANES = 128
SUBLANES = 8
VMEM_LIMIT = 56 * 1024 * 1024

DN_HEAD_DIM = 128
DN_CONV_W = 5
DN_CHUNK = 128
DN_BASE = 16
DN_HEADS_PER_STEP = 2
DN_LOCAL_CHUNKS_PER_ITER = 3
DA_HEAD_DIM = 64
DA_V_DIM = 128
LOG2_E = 1.4426950408889634


def _dot(a, b):
    return jnp.dot(a, b, preferred_element_type=F32)


def _mm(a, b):
    return jnp.dot(a.astype(BF16), b.astype(BF16), preferred_element_type=F32)


def _silu(x):
    return x * jax.nn.sigmoid(x)


def _cparams(sem):
    return pltpu.CompilerParams(dimension_semantics=sem, vmem_limit_bytes=VMEM_LIMIT)


def _const_spec(shape):
    nd = len(shape)
    return pl.BlockSpec(shape, lambda *_: (0,) * nd, pipeline_mode=pl.Buffered(1))


def _mod_kernel(c_ref, w_ref, b_ref, o_ref):
    s = _silu(c_ref[...]).astype(BF16)
    o_ref[...] = _dot(s, w_ref[...].astype(BF16)) + b_ref[...]


def _modulation(cc, ada_w, ada_b):
    depth, d, n6 = ada_w.shape
    r = cc.shape[0]
    tn = n6 // 4
    return pl.pallas_call(
        _mod_kernel,
        grid=(depth, n6 // tn),
        in_specs=[
            pl.BlockSpec((r, d), lambda l, j: (0, 0)),
            pl.BlockSpec((None, d, tn), lambda l, j: (l, 0, j)),
            pl.BlockSpec((None, 1, tn), lambda l, j: (l, 0, j)),
        ],
        out_specs=pl.BlockSpec((None, r, tn), lambda l, j: (l, 0, j)),
        out_shape=jax.ShapeDtypeStruct((depth, r, n6), F32),
        compiler_params=_cparams(("arbitrary", "arbitrary")),
        name="adaln_mod",
    )(cc, ada_w, ada_b.reshape(depth, 1, n6))


def _norm_mod(x, nw, mod, shift_idx, scale_idx):
    y = x * lax.rsqrt(jnp.mean(x * x, axis=-1, keepdims=True) + NORM_EPS) * nw
    return y * (1.0 + mod[scale_idx:scale_idx + 1, :]) + mod[shift_idx:shift_idx + 1, :]


def _row_tile(rows_per_batch):
    tm = min(512, rows_per_batch)
    assert rows_per_batch % tm == 0
    return tm


DN_PROJ_HALO = 16


def _dn_proj_kernel(x_ref, xt_ref, xb_ref, nw_ref, mod_ref, wm_ref, wg_ref, cw_ref,
                    qkv_ref, kt_ref, z_ref, g_ref, *, width, per):
    halo = DN_PROJ_HALO
    tm = x_ref.shape[0]
    pos = pl.program_id(0) % per
    first, last = pos == 0, pos == per - 1
    xx = jnp.concatenate([xt_ref[...], x_ref[...], xb_ref[...]], axis=0)
    hh = _norm_mod(xx, nw_ref[...], mod_ref[...], 0, 1).astype(BF16)
    h = hh[halo:halo + tm, :]
    tn = 512
    for j0 in range(0, 2 * width, tn):
        wide = _dot(hh, wm_ref[:, j0:j0 + tn])
        wide = jnp.concatenate([jnp.where(first, 0.0, wide[0:halo, :]), wide[halo:halo + tm, :],
                                jnp.where(last, 0.0, wide[halo + tm:, :])], axis=0)
        for j in range(j0, j0 + tn, LANES):
            win = wide[:, j - j0:j - j0 + LANES]
            acc = None
            for d in range(DN_CONV_W):
                shift = d - DN_CONV_W // 2
                moved = win if shift == 0 else pltpu.roll(win, (-shift) % win.shape[0], 0)
                term = cw_ref[d:d + 1, j:j + LANES] * moved[halo:halo + tm, :]
                acc = term if acc is None else acc + term
            y = _silu(acc)
            y = y * lax.rsqrt(jnp.sum(y * y, axis=-1, keepdims=True) + 1e-6)
            if j < width:
                y = y * (DN_HEAD_DIM ** -0.5)
            qkv_ref[:, j:j + LANES] = y
            if j >= width:
                kt_ref[j - width:j - width + LANES, :] = y.T
    for j in range(2 * width, 3 * width, tn):
        qkv_ref[:, j:j + tn] = _dot(h, wm_ref[:, j:j + tn])
    for j in range(0, width, tn):
        z_ref[:, j:j + tn] = _dot(h, wm_ref[:, 3 * width + j:3 * width + j + tn])
    g_ref[...] = _dot(h, wg_ref[...])


def _dn_proj(x2d, rows_per_batch, mod_row, nw, mods_l, w_main, w_gate, conv_w):
    r, d = x2d.shape
    width = w_main.shape[1] // 4
    gw = w_gate.shape[1]
    tm = _row_tile(rows_per_batch)
    per = rows_per_batch // tm
    halo = DN_PROJ_HALO
    hpt = tm // halo
    return pl.pallas_call(
        functools.partial(_dn_proj_kernel, width=width, per=per),
        grid=(r // tm,),
        in_specs=[
            pl.BlockSpec((tm, d), lambda i: (i, 0)),
            pl.BlockSpec((halo, d), lambda i: (jnp.maximum(i * hpt - 1, 0), 0)),
            pl.BlockSpec((halo, d), lambda i: (jnp.minimum((i + 1) * hpt, r // halo - 1), 0)),
            _const_spec((1, d)),
            pl.BlockSpec((None, 6, d), lambda i: (mod_row(i // per), 0, 0)),
            _const_spec(w_main.shape),
            _const_spec(w_gate.shape),
            _const_spec(conv_w.shape),
        ],
        out_specs=[
            pl.BlockSpec((tm, 3 * width), lambda i: (i, 0)),
            pl.BlockSpec((width, tm), lambda i: (0, i)),
            pl.BlockSpec((tm, width), lambda i: (i, 0)),
            pl.BlockSpec((tm, gw), lambda i: (i, 0)),
        ],
        out_shape=[
            jax.ShapeDtypeStruct((r, 3 * width), F32),
            jax.ShapeDtypeStruct((width, r), F32),
            jax.ShapeDtypeStruct((r, width), F32),
            jax.ShapeDtypeStruct((r, gw), F32),
        ],
        compiler_params=_cparams(("arbitrary",)),
        name="dn_proj",
    )(x2d, x2d, x2d, nw, mods_l, w_main, w_gate, conv_w)


def _cumsum_lanes(x, reverse):
    n = x.shape[1]
    lane = lax.broadcasted_iota(jnp.int32, x.shape, 1)
    s = 1
    while s < n:
        if reverse:
            x = x + jnp.where(lane < n - s, pltpu.roll(x, n - s, 1), 0.0)
        else:
            x = x + jnp.where(lane >= s, pltpu.roll(x, s, 1), 0.0)
        s *= 2
    return x


def _unit_tri_inverse(mats):
    n = mats[0].shape[0]
    row = lax.broadcasted_iota(jnp.int32, (n, n), 0)
    col = lax.broadcasted_iota(jnp.int32, (n, n), 1)

    def same_block(size):
        return (row // size) == (col // size)

    base = same_block(DN_BASE)
    negs = [-a for a in mats]
    accs = [jnp.where(base, na, 0.0) for na in negs]
    steps = int(math.log2(DN_BASE))
    ps = [_mm(p, p) for p in accs]
    for k in range(1, steps):
        if k + 1 < steps:
            both = [_mm(p, jnp.concatenate([acc, p], axis=1)) for acc, p in zip(accs, ps)]
            accs = [acc + p + b[:, :n] for acc, p, b in zip(accs, ps, both)]
            ps = [b[:, n:] for b in both]
        else:
            accs = [acc + p + _mm(p, acc) for acc, p in zip(accs, ps)]
    xs = [jnp.where(row == col, 1.0, acc).astype(BF16) for acc in accs]
    size = DN_BASE
    while size < n:
        sel = same_block(2 * size) & jnp.logical_not(same_block(size))
        offs = [jnp.where(sel, na, 0.0).astype(BF16) for na in negs]
        ys = [_dot(x, off).astype(BF16) for x, off in zip(xs, offs)]
        xs = [x + _dot(y, x).astype(BF16) for x, y in zip(xs, ys)]
        size *= 2
    return xs


def _dn_core_kernel(qc_ref, kc_ref, vc_ref, ql_ref, kl_ref, vl_ref, ktc_ref, ktl_ref, zc_ref, zl_ref, gc_ref, gl_ref,
                    cwv_ref, alog_ref, dtb_ref, onorm_ref,
                    yc_ref, yl_ref,
                    gb_ref, gbt_ref, u_ref, w_ref, qkm_ref, o_ref, st_ref, *, hb, ctx_len, lat_len):
    C = DN_CHUNK
    seq = ctx_len + lat_len
    n_ctx, n_chunks = ctx_len // C, seq // C
    width = hb * LANES

    assert 4 * hb <= SUBLANES
    raws = []
    for c in range(n_chunks):
        src = gc_ref[c * C:(c + 1) * C, :] if c < n_ctx else gl_ref[(c - n_ctx) * C:(c - n_ctx + 1) * C, :]
        raws.append(src.T[0:SUBLANES, :])
    raw = jnp.concatenate(raws, axis=0)
    grow_id = lax.broadcasted_iota(jnp.int32, raw.shape, 0) % SUBLANES
    a = raw + jnp.concatenate([dtb_ref[...]] * n_chunks, axis=0)
    g = (jnp.concatenate([-jnp.exp(alog_ref[...])] * n_chunks, axis=0)
         * (jnp.maximum(a, 0.0) + jnp.log1p(jnp.exp(-jnp.abs(a)))))
    pre = _cumsum_lanes(g, reverse=False)
    suf = _cumsum_lanes(g, reverse=True)
    vals = jnp.where(grow_id < hb, pre, jnp.where(grow_id < 2 * hb, suf, jax.nn.sigmoid(raw)))
    zero_rows = jnp.zeros((C - SUBLANES, C), F32)
    for c in range(n_chunks):
        tile = vals[c * SUBLANES:(c + 1) * SUBLANES, :]
        gbt_ref[:, c * C:(c + 1) * C] = tile
        gb_ref[c * C:(c + 1) * C, :] = jnp.concatenate([tile, zero_rows], axis=0).T

    def source(c):
        if isinstance(c, int) and c < n_ctx:
            return qc_ref, kc_ref, vc_ref, ktc_ref, c * C
        r = (c - n_ctx) * C
        return ql_ref, kl_ref, vl_ref, ktl_ref, (r if isinstance(c, int) else pl.multiple_of(r, C))

    def scratch_row(c):
        return c * C if isinstance(c, int) else pl.multiple_of(c * C, C)

    row = lax.broadcasted_iota(jnp.int32, (C, C), 0)
    col = lax.broadcasted_iota(jnp.int32, (C, C), 1)

    halo = SUBLANES
    zeros_halo = jnp.zeros((halo, LANES), F32)

    def conv_v(c, v_ref, sr, cols):
        if isinstance(c, int):
            seg_len = ctx_len if c < n_ctx else lat_len
            top = zeros_halo if sr == 0 else v_ref[sr - halo:sr, cols]
            bot = zeros_halo if sr + C == seg_len else v_ref[sr + C:sr + C + halo, cols]
        else:
            top_start = pl.multiple_of(jnp.maximum(sr - halo, 0), halo)
            bot_start = pl.multiple_of(jnp.minimum(sr + C, lat_len - halo), halo)
            top = jnp.where(sr == 0, 0.0, v_ref[pl.ds(top_start, halo), cols])
            bot = jnp.where(sr + C == lat_len, 0.0, v_ref[pl.ds(bot_start, halo), cols])
        win = jnp.concatenate([top, v_ref[pl.ds(sr, C), cols], bot], axis=0)
        acc = None
        for d in range(DN_CONV_W):
            shift = d - DN_CONV_W // 2
            moved = win if shift == 0 else pltpu.roll(win, (-shift) % win.shape[0], 0)
            term = cwv_ref[d:d + 1, cols] * moved[halo:halo + C, :]
            acc = term if acc is None else acc + term
        return _silu(acc)

    def local_chunks(chunk_ids):
        chains, a_mats, rhss = [], [], []
        for c in chunk_ids:
            q_ref, k_ref, v_ref, kt_ref, sr = source(c)
            r = scratch_row(c)
            gb = gb_ref[pl.ds(r, C), :]
            gbt = gbt_ref[:, pl.ds(r, C)]
            for j in range(hb):
                cols = slice(j * LANES, (j + 1) * LANES)
                qn = q_ref[pl.ds(sr, C), cols]
                kn = k_ref[pl.ds(sr, C), cols]
                vn = conv_v(c, v_ref, sr, cols)
                knt = kt_ref[cols, pl.ds(sr, C)].astype(BF16)
                kk = _dot(kn.astype(BF16), knt)
                qk = _dot(qn.astype(BF16), knt)
                for dr in range(2):
                    gl_, bl_ = dr * hb + j, 2 * hb + dr * hb + j
                    gcol, grow, bcol = gb[:, gl_:gl_ + 1], gbt[gl_:gl_ + 1, :], gb[:, bl_:bl_ + 1]
                    incl = (row >= col) if dr == 0 else (row <= col)
                    strict = (row > col) if dr == 0 else (row < col)
                    dec = jnp.where(incl, jnp.exp(jnp.where(incl, gcol - grow, 0.0)), 0.0)
                    qkm_ref[dr, pl.ds(r, C), cols] = (qk * dec).astype(BF16)
                    chains.append((dr, r, cols))
                    a_mats.append(jnp.where(strict, kk * dec * bcol, 0.0))
                    rhss.append(jnp.concatenate([vn * bcol, kn * (bcol * jnp.exp(gcol))], axis=1))
        tinvs = _unit_tri_inverse(a_mats)
        uws = [_dot(tinv, rhs.astype(BF16)) for tinv, rhs in zip(tinvs, rhss)]
        for (dr, r, cols), uw in zip(chains, uws):
            u_ref[dr, pl.ds(r, C), cols] = uw[:, :LANES]
            w_ref[dr, pl.ds(r, C), cols] = uw[:, LANES:].astype(BF16)

    per_iter = max(p for p in range(1, DN_LOCAL_CHUNKS_PER_ITER + 1) if n_chunks % p == 0)
    n_groups = n_chunks // per_iter
    assert n_ctx <= per_iter
    local_chunks(list(range(per_iter)))

    def local_body(it, carry):
        local_chunks([it * per_iter + sub for sub in range(per_iter)])
        return carry

    lax.fori_loop(1, n_groups, local_body, 0)

    st_ref[...] = jnp.zeros(st_ref.shape, F32)
    o_ref[...] = jnp.zeros((seq, width), F32)

    def scan_step(s):
        chains = []
        for dr in range(2):
            if dr == 0:
                c = s
            elif isinstance(s, int):
                c = n_ctx - 1 - s if s < n_ctx else n_chunks - 1 - s + n_ctx
            else:
                c = n_chunks - 1 - s + n_ctx
            q_ref, _, _, kt_ref, sr = source(c)
            r = scratch_row(c)
            gb = gb_ref[pl.ds(r, C), :]
            gbt = gbt_ref[:, pl.ds(r, C)]
            for j in range(hb):
                cols = slice(j * LANES, (j + 1) * LANES)
                gl_ = dr * hb + j
                gcol, grow = gb[:, gl_:gl_ + 1], gbt[gl_:gl_ + 1, :]
                g_end_c = gcol[C - 1:C, :] if dr == 0 else gcol[0:1, :]
                g_end_r = grow[:, C - 1:C] if dr == 0 else grow[:, 0:1]
                qd = (q_ref[pl.ds(sr, C), cols] * jnp.exp(gcol)).astype(BF16)
                kdt = (kt_ref[cols, pl.ds(sr, C)] * jnp.exp(g_end_r - grow)).astype(BF16)
                chains.append((dr, j, r, cols, qd, kdt, jnp.exp(g_end_c)))
        states = [st_ref[dr * hb + j] for dr, j, *_ in chains]
        sb = [st.astype(BF16) for st in states]
        wss = [_dot(w_ref[dr, pl.ds(r, C), cols], s16)
               for (dr, j, r, cols, qd, kdt, gt), s16 in zip(chains, sb)]
        nus = [(u_ref[dr, pl.ds(r, C), cols] - ws).astype(BF16)
               for (dr, j, r, cols, qd, kdt, gt), ws in zip(chains, wss)]
        upds = [_dot(kdt, nu) for (dr, j, r, cols, qd, kdt, gt), nu in zip(chains, nus)]
        outs = [_dot(qd, s16) + _dot(qkm_ref[dr, pl.ds(r, C), cols], nu)
                for (dr, j, r, cols, qd, kdt, gt), s16, nu in zip(chains, sb, nus)]
        for (dr, j, r, cols, qd, kdt, gt), st, upd, out in zip(chains, states, upds, outs):
            st_ref[dr * hb + j] = st * gt + upd
            o_ref[pl.ds(r, C), cols] += out

    for s in range(n_ctx):
        scan_step(s)

    def scan_body(s, carry):
        scan_step(s)
        return carry

    lax.fori_loop(n_ctx, n_chunks, scan_body, 0, unroll=4)

    onorm = onorm_ref[...]
    tr = 256
    for r0 in range(0, seq, tr):
        for j in range(hb):
            cols = slice(j * LANES, (j + 1) * LANES)
            o = o_ref[r0:r0 + tr, cols]
            y = o * lax.rsqrt(jnp.mean(o * o, axis=-1, keepdims=True) + NORM_EPS) * onorm
            if r0 < ctx_len:
                yc_ref[r0:r0 + tr, cols] = (y * _silu(zc_ref[r0:r0 + tr, cols])).astype(BF16)
            else:
                q0 = r0 - ctx_len
                yl_ref[q0:q0 + tr, cols] = (y * _silu(zl_ref[q0:q0 + tr, cols])).astype(BF16)


def _dn_core(qkv_c, qkv_l, kt_c, kt_l, z_c, z_l, g_c, g_l, conv_w, alog_g, dtb_g, onorm, batch, ctx_len, lat_len):
    hb = DN_HEADS_PER_STEP
    width = hb * LANES
    dn_width = z_c.shape[1]
    ng = dn_width // width
    seq = ctx_len + lat_len
    assert ctx_len % 256 == 0 and lat_len % 256 == 0 and ctx_len % DN_CHUNK == 0

    def col_spec(rows, off):
        return pl.BlockSpec((rows, width), lambda b, g: (b, off + g))

    in_specs = [
        col_spec(ctx_len, 0), col_spec(ctx_len, ng), col_spec(ctx_len, 2 * ng),
        col_spec(lat_len, 0), col_spec(lat_len, ng), col_spec(lat_len, 2 * ng),
        pl.BlockSpec((width, ctx_len), lambda b, g: (g, b)),
        pl.BlockSpec((width, lat_len), lambda b, g: (g, b)),
        col_spec(ctx_len, 0), col_spec(lat_len, 0),
        pl.BlockSpec((ctx_len, LANES), lambda b, g: (b, g)),
        pl.BlockSpec((lat_len, LANES), lambda b, g: (b, g)),
        pl.BlockSpec((SUBLANES, width), lambda b, g: (0, 2 * ng + g)),
        pl.BlockSpec((None, SUBLANES, LANES), lambda b, g: (g, 0, 0)),
        pl.BlockSpec((None, SUBLANES, LANES), lambda b, g: (g, 0, 0)),
        pl.BlockSpec((1, LANES), lambda b, g: (0, 0)),
    ]
    scratch = [
        pltpu.VMEM((seq, LANES), F32),
        pltpu.VMEM((SUBLANES, seq), F32),
        pltpu.VMEM((2, seq, width), F32),
        pltpu.VMEM((2, seq, width), BF16),
        pltpu.VMEM((2, seq, width), BF16),
        pltpu.VMEM((seq, width), F32),
        pltpu.VMEM((2 * hb, DN_HEAD_DIM, DN_HEAD_DIM), F32),
    ]
    return pl.pallas_call(
        functools.partial(_dn_core_kernel, hb=hb, ctx_len=ctx_len, lat_len=lat_len),
        grid=(batch, ng),
        in_specs=in_specs,
        out_specs=[col_spec(ctx_len, 0), col_spec(lat_len, 0)],
        out_shape=[jax.ShapeDtypeStruct((batch * ctx_len, dn_width), BF16),
                   jax.ShapeDtypeStruct((batch * lat_len, dn_width), BF16)],
        scratch_shapes=scratch,
        compiler_params=_cparams(("arbitrary", "arbitrary")),
        name="dn_core",
    )(qkv_c, qkv_c, qkv_c, qkv_l, qkv_l, qkv_l, kt_c, kt_l, z_c, z_l, g_c, g_l, conv_w, alog_g, dtb_g, onorm)


def _da_proj_kernel(*refs, rope, qk_width, v_width):
    if rope:
        x_ref, nw_ref, mod_ref, w_ref, cos_ref, sin_ref, q_ref, k_ref, v_ref = refs
    else:
        x_ref, nw_ref, mod_ref, w_ref, q_ref, k_ref, v_ref = refs
    h = _norm_mod(x_ref[...], nw_ref[...], mod_ref[...], 0, 1).astype(BF16)
    tm = h.shape[0]
    if rope:
        cos, sin = cos_ref[...], sin_ref[...]
        lane = lax.broadcasted_iota(jnp.int32, (tm, LANES), 1)
        first_half = (lane % DA_HEAD_DIM) < (DA_HEAD_DIM // 2)
    tn = 512
    for which, dst in ((0, q_ref), (1, k_ref)):
        for j0 in range(0, qk_width, tn):
            wide = _dot(h, w_ref[:, which * qk_width + j0:which * qk_width + j0 + tn])
            for j in range(0, tn, LANES):
                y = wide[:, j:j + LANES]
                if which == 0:
                    y = y * (DA_HEAD_DIM ** -0.5 * LOG2_E)
                if rope:
                    half = DA_HEAD_DIM // 2
                    partner = jnp.where(first_half, pltpu.roll(y, LANES - half, 1), pltpu.roll(y, half, 1))
                    y = y * cos + partner * sin
                dst[:, j0 + j:j0 + j + LANES] = y.astype(BF16)
    for j in range(0, v_width, 512):
        v_ref[:, j:j + 512] = _dot(h, w_ref[:, 2 * qk_width + j:2 * qk_width + j + 512]).astype(BF16)


def _da_proj(x2d, rows_per_batch, mod_row, nw, mods_l, w_qkv, rope_tabs, qk_width, v_width):
    r, d = x2d.shape
    tm = _row_tile(rows_per_batch)
    per = rows_per_batch // tm
    in_specs = [
        pl.BlockSpec((tm, d), lambda i: (i, 0)),
        _const_spec((1, d)),
        pl.BlockSpec((None, 6, d), lambda i: (mod_row(i // per), 0, 0)),
        _const_spec(w_qkv.shape),
    ]
    args = [x2d, nw, mods_l, w_qkv]
    if rope_tabs is not None:
        in_specs += [pl.BlockSpec((tm, LANES), lambda i: (i % per, 0))] * 2
        args += list(rope_tabs)
    return pl.pallas_call(
        functools.partial(_da_proj_kernel, rope=rope_tabs is not None, qk_width=qk_width, v_width=v_width),
        grid=(r // tm,),
        in_specs=in_specs,
        out_specs=[pl.BlockSpec((tm, qk_width), lambda i: (i, 0)),
                   pl.BlockSpec((tm, qk_width), lambda i: (i, 0)),
                   pl.BlockSpec((tm, v_width), lambda i: (i, 0))],
        out_shape=[jax.ShapeDtypeStruct((r, qk_width), BF16),
                   jax.ShapeDtypeStruct((r, qk_width), BF16),
                   jax.ShapeDtypeStruct((r, v_width), BF16)],
        compiler_params=_cparams(("arbitrary",)),
        name="da_proj",
    )(*args)


def _rope_tables(n_tokens):
    rows = n_tokens // GRID_W
    row = jnp.repeat(jnp.arange(rows, dtype=jnp.int32), GRID_W).astype(F32)
    col = jnp.tile(jnp.arange(GRID_W, dtype=jnp.int32), rows).astype(F32)
    n_freq = DA_HEAD_DIM // 4
    inv = ROPE_THETA ** (-jnp.arange(n_freq, dtype=F32) / n_freq)
    ang = jnp.concatenate([row[:, None] * inv, col[:, None] * inv], axis=-1)
    cos, sin = jnp.cos(ang), jnp.sin(ang)
    reps = LANES // DA_HEAD_DIM
    return (jnp.tile(jnp.concatenate([cos, cos], axis=-1), (1, reps)),
            jnp.tile(jnp.concatenate([-sin, sin], axis=-1), (1, reps)))


def _flash_kernel(*refs, n_seg, seg_lens, tk, sub_rows, lambda_init):
    q_ref = refs[0]
    k_refs = refs[1:1 + n_seg]
    v_refs = refs[1 + n_seg:1 + 2 * n_seg]
    lam_ref, subln_ref, y_ref, s_ref, mx_ref, sum_ref = refs[1 + 2 * n_seg:]
    tq = q_ref.shape[0]
    lam = lam_ref[...]
    lam_full = (jnp.exp(jnp.sum(lam[0:1, :] * lam[1:2, :], axis=-1, keepdims=True))
                - jnp.exp(jnp.sum(lam[2:3, :] * lam[3:4, :], axis=-1, keepdims=True)) + lambda_init)
    lane = lax.broadcasted_iota(jnp.int32, (1, LANES), 1)
    map_masks = (lane < DA_HEAD_DIM, lane >= DA_HEAD_DIM)
    tiles = [(sg, r) for sg in range(n_seg) for r in range(0, seg_lens[sg], tk)]
    halves = [slice(h * LANES, (h + 1) * LANES) for h in range(tk // LANES)]
    sub = min(sub_rows, tq)

    def pass_a(rows):
        q = q_ref[rows, :]
        for mp in range(2):
            qz = jnp.where(map_masks[mp], q, jnp.zeros_like(q))
            run_max = None
            for t, (sg, r) in enumerate(tiles):
                k = k_refs[sg][r:r + tk, :]
                s = lax.dot_general(qz, k, (((1,), (1,)), ((), ())), preferred_element_type=F32)
                s_ref[mp, t, rows, :] = s
                for hs in halves:
                    run_max = s[:, hs] if run_max is None else jnp.maximum(run_max, s[:, hs])
            mx_ref[mp, rows, :] = jnp.broadcast_to(jnp.max(run_max, axis=-1, keepdims=True), (sub, LANES))

    def pass_b(rows):
        rb = 64
        for mp in range(2):
            for r0 in range(rows.start, rows.stop, rb):
                blk = slice(r0, r0 + rb)
                row_max = jnp.concatenate([mx_ref[mp, blk, :]] * len(halves), axis=1)
                run_sum = None
                for t in range(len(tiles)):
                    e = jnp.exp2(s_ref[mp, t, blk, :] - row_max)
                    s_ref[mp, t, blk, :] = e
                    for hs in halves:
                        run_sum = e[:, hs] if run_sum is None else run_sum + e[:, hs]
                total = jnp.sum(run_sum, axis=-1, keepdims=True)
                coef = (1.0 / total) if mp == 0 else (lam_full / total)
                sum_ref[mp, blk, :] = jnp.broadcast_to(coef, (rb, LANES))

    def pass_c(rows):
        c0 = jnp.concatenate([sum_ref[0, rows, :]] * len(halves), axis=1)
        c1 = jnp.concatenate([sum_ref[1, rows, :]] * len(halves), axis=1)
        acc = None
        for t, (sg, r) in enumerate(tiles):
            p = (s_ref[0, t, rows, :] * c0 - s_ref[1, t, rows, :] * c1).astype(BF16)
            part = _dot(p, v_refs[sg][r:r + tk, :])
            acc = part if acc is None else acc + part
        y = acc * lax.rsqrt(jnp.mean(acc * acc, axis=-1, keepdims=True) + NORM_EPS) * subln_ref[...]
        y_ref[rows, :] = (y * (1.0 - lambda_init)).astype(BF16)

    subs = [slice(r0, r0 + sub) for r0 in range(0, tq, sub)]
    pass_a(subs[0])
    for u in range(1, len(subs)):
        pass_a(subs[u])
        pass_b(subs[u - 1])
        pass_c(subs[u - 1])
    pass_b(subs[-1])
    pass_c(subs[-1])


def _flash(q, ks, vs, seg_lens, q_rows_per_batch, lam, subln, lambda_init, batch, heads):
    tq = min(2048, q_rows_per_batch)
    tk = 256
    nq = q_rows_per_batch // tq
    n_seg = len(ks)
    n_tiles = sum(n // tk for n in seg_lens)
    assert all(n % tk == 0 for n in seg_lens)
    in_specs = [pl.BlockSpec((tq, LANES), lambda b, h, i: (b * nq + i, h))]
    for n in list(seg_lens) * 2:
        in_specs.append(pl.BlockSpec((n, LANES), lambda b, h, i: (b, h)))
    in_specs += [pl.BlockSpec(lam.shape, lambda b, h, i: (0, 0)), pl.BlockSpec((1, LANES), lambda b, h, i: (0, 0))]
    return pl.pallas_call(
        functools.partial(_flash_kernel, n_seg=n_seg, seg_lens=tuple(seg_lens), tk=tk, sub_rows=256,
                          lambda_init=lambda_init),
        grid=(batch, heads, nq),
        in_specs=in_specs,
        out_specs=pl.BlockSpec((tq, LANES), lambda b, h, i: (b * nq + i, h)),
        out_shape=jax.ShapeDtypeStruct((q.shape[0], heads * DA_V_DIM), BF16),
        scratch_shapes=[pltpu.VMEM((2, n_tiles, tq, tk), F32),
                        pltpu.VMEM((2, tq, LANES), F32),
                        pltpu.VMEM((2, tq, LANES), F32)],
        compiler_params=_cparams(("arbitrary", "arbitrary", "arbitrary")),
        name="diff_flash",
    )(q, *ks, *vs, lam, subln)


def _post_kernel(*refs, final, ff_tile):
    if final:
        x_ref, y_ref, mod_ref, wo_ref, nw_ref, w1_ref, w2_ref, fw_ref, o_ref, acc_ref = refs
    else:
        x_ref, y_ref, mod_ref, wo_ref, nw_ref, w1_ref, w2_ref, o_ref, acc_ref = refs
    mod = mod_ref[...]
    x1 = x_ref[...] + mod[2:3, :] * _dot(y_ref[...], wo_ref[...])
    h = _norm_mod(x1, nw_ref[...], mod, 3, 4).astype(BF16)
    d_ff = w1_ref.shape[1]
    for f in range(0, d_ff, ff_tile):
        a = jnp.maximum(_dot(h, w1_ref[:, f:f + ff_tile]), 0.0)
        part = _dot((a * a).astype(BF16), w2_ref[f:f + ff_tile, :])
        if f == 0:
            acc_ref[...] = part
        else:
            acc_ref[...] += part
    x2 = x1 + mod[5:6, :] * acc_ref[...]
    if final:
        x2 = x2 * lax.rsqrt(jnp.mean(x2 * x2, axis=-1, keepdims=True) + NORM_EPS) * fw_ref[...]
    o_ref[...] = x2


def _post(x2d, y2d, rows_per_batch, mod_row, mods_l, w_out, nw, w1, w2, final_w):
    r, d = x2d.shape
    tm = _row_tile(rows_per_batch)
    per = rows_per_batch // tm
    final = final_w is not None
    in_specs = [
        pl.BlockSpec((tm, d), lambda i: (i, 0)),
        pl.BlockSpec((tm, y2d.shape[1]), lambda i: (i, 0)),
        pl.BlockSpec((None, 6, d), lambda i: (mod_row(i // per), 0, 0)),
        _const_spec(w_out.shape),
        _const_spec((1, d)),
        _const_spec(w1.shape),
        _const_spec(w2.shape),
    ]
    args = [x2d, y2d, mods_l, w_out, nw, w1, w2]
    if final:
        in_specs.append(_const_spec((1, d)))
        args.append(final_w)
    return pl.pallas_call(
        functools.partial(_post_kernel, final=final, ff_tile=1024),
        grid=(r // tm,),
        in_specs=in_specs,
        out_specs=pl.BlockSpec((tm, d), lambda i: (i, 0)),
        out_shape=jax.ShapeDtypeStruct((r, d), F32),
        scratch_shapes=[pltpu.VMEM((tm, d), F32)],
        compiler_params=_cparams(("arbitrary",)),
        name="post_mlp",
    )(*args)


def _dn_gate_layout(n_heads, hb):
    ng = n_heads // hb
    idx = np.full((ng, LANES), -1, np.int64)
    for g in range(ng):
        for kind in range(4):
            for j in range(hb):
                idx[g, kind * hb + j] = kind * n_heads + g * hb + j
    return idx.reshape(-1)


def kernel(x, c, ctx, c_ctx, ada_w, ada_b, norm_w, mlp_w1, mlp_w2, dn_w_in, dn_conv, dn_a_log, dn_dt_bias,
           dn_out_norm, dn_w_out, da_w_qkv, da_lambda, da_subln, da_w_out, final_norm):
    batch, lat_len, d = x.shape
    ctx_len = ctx.shape[1]
    depth = ada_w.shape[0]
    dn_width = dn_w_out.shape[1]
    dn_heads = dn_width // DN_HEAD_DIM
    da_v_width = da_w_out.shape[1]
    da_heads = da_v_width // DA_V_DIM
    da_qk_width = da_heads * 2 * DA_HEAD_DIM
    hb = DN_HEADS_PER_STEP

    n_rows = -(-(batch + 1) // SUBLANES) * SUBLANES
    cc = jnp.zeros((n_rows, d), F32).at[:batch].set(c).at[batch].set(c_ctx)
    mods = _modulation(cc, ada_w, ada_b).reshape(depth, n_rows, 6, d)
    lat_row = lambda b: b
    ctx_row = lambda b: batch

    xl = x.reshape(batch * lat_len, d)
    xc = ctx.reshape(batch * ctx_len, d)

    gate_idx = _dn_gate_layout(dn_heads, hb)
    gate_valid = jnp.asarray(gate_idx >= 0)
    gate_src = jnp.asarray(np.maximum(gate_idx, 0))
    ng = dn_heads // hb
    rope_tabs = _rope_tables(lat_len)

    for i in range(depth):
        last = i == depth - 1
        mods_l = mods[i]
        j = i // 2
        if i % 2 == 0:
            w_in = dn_w_in[j]
            w_main = w_in[:, :4 * dn_width].astype(BF16)
            w_gate = jnp.where(gate_valid[None, :], jnp.take(w_in[:, 4 * dn_width:], gate_src, axis=1), 0.0).astype(BF16)
            conv_w = jnp.zeros((SUBLANES, 3 * dn_width), F32).at[:DN_CONV_W].set(dn_conv[j])
            def gate_rows(p):
                rows = jnp.transpose(p.reshape(2, ng, hb), (1, 0, 2)).reshape(ng, 2 * hb, 1)
                return jnp.zeros((ng, SUBLANES, LANES), F32).at[:, :2 * hb, :].set(
                    jnp.broadcast_to(rows, (ng, 2 * hb, LANES)))

            alog_g, dtb_g = gate_rows(dn_a_log[j]), gate_rows(dn_dt_bias[j])
            nw0 = norm_w[i, 0].reshape(1, d)
            qkv_l, kt_l, z_l, g_l = _dn_proj(xl, lat_len, lat_row, nw0, mods_l, w_main, w_gate, conv_w)
            qkv_c, kt_c, z_c, g_c = _dn_proj(xc, ctx_len, ctx_row, nw0, mods_l, w_main, w_gate, conv_w)
            y_c, y_l = _dn_core(qkv_c, qkv_l, kt_c, kt_l, z_c, z_l, g_c, g_l, conv_w, alog_g, dtb_g,
                                dn_out_norm[j].reshape(1, DN_HEAD_DIM), batch, ctx_len, lat_len)
            w_out = dn_w_out[j].astype(BF16)
        else:
            lambda_init = 0.8 - 0.6 * math.exp(-0.3 * i)
            w_qkv = da_w_qkv[j].astype(BF16)
            nw0 = norm_w[i, 0].reshape(1, d)
            q_l, k_l, v_l = _da_proj(xl, lat_len, lat_row, nw0, mods_l, w_qkv, rope_tabs, da_qk_width, da_v_width)
            q_c, k_c, v_c = _da_proj(xc, ctx_len, ctx_row, nw0, mods_l, w_qkv, None, da_qk_width, da_v_width)
            lam = da_lambda[j]
            subln = da_subln[j].reshape(1, DA_V_DIM)
            y_l = _flash(q_l, (k_c, k_l), (v_c, v_l), (ctx_len, lat_len), lat_len, lam, subln, lambda_init,
                         batch, da_heads)
            y_c = None if last else _flash(q_c, (k_c,), (v_c,), (ctx_len,), ctx_len, lam, subln, lambda_init,
                                           batch, da_heads)
            w_out = da_w_out[j].astype(BF16)
        nw1 = norm_w[i, 1].reshape(1, d)
        w1 = mlp_w1[i].astype(BF16)
        w2 = mlp_w2[i].astype(BF16)
        xl = _post(xl, y_l, lat_len, lat_row, mods_l, w_out, nw1, w1, w2,
                   final_norm.reshape(1, d) if last else None)
        if not last:
            xc = _post(xc, y_c, ctx_len, ctx_row, mods_l, w_out, nw1, w1, w2, None)
    return xl.reshape(batch, lat_len, d)
```

```python
import functools
import math

import jax
import jax.numpy as jnp
import numpy as np
from jax import lax
from jax.experimental import pallas as pl
from jax.experimental.pallas import tpu as pltpu

F32 = jnp.float32
BF16 = jnp.bfloat16

NORM_EPS = 1e-6
ROPE_THETA = 10000.0
GRID_W = 64
LANES = 128
SUBLANES = 8
VMEM_LIMIT = 56 * 1024 * 1024

DN_HEAD_DIM = 128
DN_CONV_W = 5
DN_CHUNK = 128
DN_BASE = 16
DN_HEADS_PER_STEP = 2
DN_LOCAL_CHUNKS_PER_ITER = 3
DA_HEAD_DIM = 64
DA_V_DIM = 128
LOG2_E = 1.4426950408889634


def _dot(a, b):
    return jnp.dot(a, b, preferred_element_type=F32)


def _mm(a, b):
    return jnp.dot(a.astype(BF16), b.astype(BF16), preferred_element_type=F32)


def _silu(x):
    return x * jax.nn.sigmoid(x)


def _cparams(sem):
    return pltpu.CompilerParams(dimension_semantics=sem, vmem_limit_bytes=VMEM_LIMIT)


def _const_spec(shape):
    nd = len(shape)
    return pl.BlockSpec(shape, lambda *_: (0,) * nd, pipeline_mode=pl.Buffered(1))


def _mod_kernel(c_ref, w_ref, b_ref, o_ref):
    s = _silu(c_ref[...]).astype(BF16)
    o_ref[...] = _dot(s, w_ref[...].astype(BF16)) + b_ref[...]


def _modulation(cc, ada_w, ada_b):
    depth, d, n6 = ada_w.shape
    r = cc.shape[0]
    tn = n6 // 4
    return pl.pallas_call(
        _mod_kernel,
        grid=(depth, n6 // tn),
        in_specs=[
            pl.BlockSpec((r, d), lambda l, j: (0, 0)),
            pl.BlockSpec((None, d, tn), lambda l, j: (l, 0, j)),
            pl.BlockSpec((None, 1, tn), lambda l, j: (l, 0, j)),
        ],
        out_specs=pl.BlockSpec((None, r, tn), lambda l, j: (l, 0, j)),
        out_shape=jax.ShapeDtypeStruct((depth, r, n6), F32),
        compiler_params=_cparams(("arbitrary", "arbitrary")),
        name="adaln_mod",
    )(cc, ada_w, ada_b.reshape(depth, 1, n6))


def _norm_mod(x, nw, mod, shift_idx, scale_idx):
    y = x * lax.rsqrt(jnp.mean(x * x, axis=-1, keepdims=True) + NORM_EPS) * nw
    return y * (1.0 + mod[scale_idx:scale_idx + 1, :]) + mod[shift_idx:shift_idx + 1, :]


def _row_tile(rows_per_batch):
    tm = min(512, rows_per_batch)
    assert rows_per_batch % tm == 0
    return tm


DN_PROJ_HALO = 16


def _dn_proj_kernel(x_ref, xt_ref, xb_ref, nw_ref, mod_ref, wm_ref, wg_ref, cw_ref,
                    qkv_ref, kt_ref, z_ref, g_ref, *, width, per):
    halo = DN_PROJ_HALO
    tm = x_ref.shape[0]
    pos = pl.program_id(0) % per
    first, last = pos == 0, pos == per - 1
    xx = jnp.concatenate([xt_ref[...], x_ref[...], xb_ref[...]], axis=0)
    hh = _norm_mod(xx, nw_ref[...], mod_ref[...], 0, 1).astype(BF16)
    h = hh[halo:halo + tm, :]
    tn = 512
    for j0 in range(0, 2 * width, tn):
        wide = _dot(hh, wm_ref[:, j0:j0 + tn])
        wide = jnp.concatenate([jnp.where(first, 0.0, wide[0:halo, :]), wide[halo:halo + tm, :],
                                jnp.where(last, 0.0, wide[halo + tm:, :])], axis=0)
        for j in range(j0, j0 + tn, LANES):
            win = wide[:, j - j0:j - j0 + LANES]
            acc = None
            for d in range(DN_CONV_W):
                shift = d - DN_CONV_W // 2
                moved = win if shift == 0 else pltpu.roll(win, (-shift) % win.shape[0], 0)
                term = cw_ref[d:d + 1, j:j + LANES] * moved[halo:halo + tm, :]
                acc = term if acc is None else acc + term
            y = _silu(acc)
            y = y * lax.rsqrt(jnp.sum(y * y, axis=-1, keepdims=True) + 1e-6)
            if j < width:
                y = y * (DN_HEAD_DIM ** -0.5)
            qkv_ref[:, j:j + LANES] = y
            if j >= width:
                kt_ref[j - width:j - width + LANES, :] = y.T
    for j in range(2 * width, 3 * width, tn):
        qkv_ref[:, j:j + tn] = _dot(h, wm_ref[:, j:j + tn])
    for j in range(0, width, tn):
        z_ref[:, j:j + tn] = _dot(h, wm_ref[:, 3 * width + j:3 * width + j + tn])
    g_ref[...] = _dot(h, wg_ref[...])


def _dn_proj(x2d, rows_per_batch, mod_row, nw, mods_l, w_main, w_gate, conv_w):
    r, d = x2d.shape
    width = w_main.shape[1] // 4
    gw = w_gate.shape[1]
    tm = _row_tile(rows_per_batch)
    per = rows_per_batch // tm
    halo = DN_PROJ_HALO
    hpt = tm // halo
    return pl.pallas_call(
        functools.partial(_dn_proj_kernel, width=width, per=per),
        grid=(r // tm,),
        in_specs=[
            pl.BlockSpec((tm, d), lambda i: (i, 0)),
            pl.BlockSpec((halo, d), lambda i: (jnp.maximum(i * hpt - 1, 0), 0)),
            pl.BlockSpec((halo, d), lambda i: (jnp.minimum((i + 1) * hpt, r // halo - 1), 0)),
            _const_spec((1, d)),
            pl.BlockSpec((None, 6, d), lambda i: (mod_row(i // per), 0, 0)),
            _const_spec(w_main.shape),
            _const_spec(w_gate.shape),
            _const_spec(conv_w.shape),
        ],
        out_specs=[
            pl.BlockSpec((tm, 3 * width), lambda i: (i, 0)),
            pl.BlockSpec((width, tm), lambda i: (0, i)),
            pl.BlockSpec((tm, width), lambda i: (i, 0)),
            pl.BlockSpec((tm, gw), lambda i: (i, 0)),
        ],
        out_shape=[
            jax.ShapeDtypeStruct((r, 3 * width), F32),
            jax.ShapeDtypeStruct((width, r), F32),
            jax.ShapeDtypeStruct((r, width), F32),
            jax.ShapeDtypeStruct((r, gw), F32),
        ],
        compiler_params=_cparams(("arbitrary",)),
        name="dn_proj",
    )(x2d, x2d, x2d, nw, mods_l, w_main, w_gate, conv_w)


def _cumsum_lanes(x, reverse):
    n = x.shape[1]
    lane = lax.broadcasted_iota(jnp.int32, x.shape, 1)
    s = 1
    while s < n:
        if reverse:
            x = x + jnp.where(lane < n - s, pltpu.roll(x, n - s, 1), 0.0)
        else:
            x = x + jnp.where(lane >= s, pltpu.roll(x, s, 1), 0.0)
        s *= 2
    return x


def _unit_tri_inverse(mats):
    n = mats[0].shape[0]
    row = lax.broadcasted_iota(jnp.int32, (n, n), 0)
    col = lax.broadcasted_iota(jnp.int32, (n, n), 1)

    def same_block(size):
        return (row // size) == (col // size)

    base = same_block(DN_BASE)
    negs = [-a for a in mats]
    accs = [jnp.where(base, na, 0.0) for na in negs]
    steps = int(math.log2(DN_BASE))
    ps = [_mm(p, p) for p in accs]
    for k in range(1, steps):
        if k + 1 < steps:
            both = [_mm(p, jnp.concatenate([acc, p], axis=1)) for acc, p in zip(accs, ps)]
            accs = [acc + p + b[:, :n] for acc, p, b in zip(accs, ps, both)]
            ps = [b[:, n:] for b in both]
        else:
            accs = [acc + p + _mm(p, acc) for acc, p in zip(accs, ps)]
    xs = [jnp.where(row == col, 1.0, acc).astype(BF16) for acc in accs]
    size = DN_BASE
    while size < n:
        sel = same_block(2 * size) & jnp.logical_not(same_block(size))
        offs = [jnp.where(sel, na, 0.0).astype(BF16) for na in negs]
        ys = [_dot(x, off).astype(BF16) for x, off in zip(xs, offs)]
        xs = [x + _dot(y, x).astype(BF16) for x, y in zip(xs, ys)]
        size *= 2
    return xs


def _dn_core_kernel(qc_ref, kc_ref, vc_ref, ql_ref, kl_ref, vl_ref, ktc_ref, ktl_ref, zc_ref, zl_ref, gc_ref, gl_ref,
                    cwv_ref, alog_ref, dtb_ref, onorm_ref,
                    yc_ref, yl_ref,
                    gb_ref, gbt_ref, u_ref, w_ref, qkm_ref, o_ref, st_ref, *, hb, ctx_len, lat_len):
    C = DN_CHUNK
    seq = ctx_len + lat_len
    n_ctx, n_chunks = ctx_len // C, seq // C
    width = hb * LANES

    assert 4 * hb <= SUBLANES
    raws = []
    for c in range(n_chunks):
        src = gc_ref[c * C:(c + 1) * C, :] if c < n_ctx else gl_ref[(c - n_ctx) * C:(c - n_ctx + 1) * C, :]
        raws.append(src.T[0:SUBLANES, :])
    raw = jnp.concatenate(raws, axis=0)
    grow_id = lax.broadcasted_iota(jnp.int32, raw.shape, 0) % SUBLANES
    a = raw + jnp.concatenate([dtb_ref[...]] * n_chunks, axis=0)
    g = (jnp.concatenate([-jnp.exp(alog_ref[...])] * n_chunks, axis=0)
         * (jnp.maximum(a, 0.0) + jnp.log1p(jnp.exp(-jnp.abs(a)))))
    pre = _cumsum_lanes(g, reverse=False)
    suf = _cumsum_lanes(g, reverse=True)
    vals = jnp.where(grow_id < hb, pre, jnp.where(grow_id < 2 * hb, suf, jax.nn.sigmoid(raw)))
    zero_rows = jnp.zeros((C - SUBLANES, C), F32)
    for c in range(n_chunks):
        tile = vals[c * SUBLANES:(c + 1) * SUBLANES, :]
        gbt_ref[:, c * C:(c + 1) * C] = tile
        gb_ref[c * C:(c + 1) * C, :] = jnp.concatenate([tile, zero_rows], axis=0).T

    def source(c):
        if isinstance(c, int) and c < n_ctx:
            return qc_ref, kc_ref, vc_ref, ktc_ref, c * C
        r = (c - n_ctx) * C
        return ql_ref, kl_ref, vl_ref, ktl_ref, (r if isinstance(c, int) else pl.multiple_of(r, C))

    def scratch_row(c):
        return c * C if isinstance(c, int) else pl.multiple_of(c * C, C)

    row = lax.broadcasted_iota(jnp.int32, (C, C), 0)
    col = lax.broadcasted_iota(jnp.int32, (C, C), 1)

    halo = SUBLANES
    zeros_halo = jnp.zeros((halo, LANES), F32)

    def conv_v(c, v_ref, sr, cols):
        if isinstance(c, int):
            seg_len = ctx_len if c < n_ctx else lat_len
            top = zeros_halo if sr == 0 else v_ref[sr - halo:sr, cols]
            bot = zeros_halo if sr + C == seg_len else v_ref[sr + C:sr + C + halo, cols]
        else:
            top_start = pl.multiple_of(jnp.maximum(sr - halo, 0), halo)
            bot_start = pl.multiple_of(jnp.minimum(sr + C, lat_len - halo), halo)
            top = jnp.where(sr == 0, 0.0, v_ref[pl.ds(top_start, halo), cols])
            bot = jnp.where(sr + C == lat_len, 0.0, v_ref[pl.ds(bot_start, halo), cols])
        win = jnp.concatenate([top, v_ref[pl.ds(sr, C), cols], bot], axis=0)
        acc = None
        for d in range(DN_CONV_W):
            shift = d - DN_CONV_W // 2
            moved = win if shift == 0 else pltpu.roll(win, (-shift) % win.shape[0], 0)
            term = cwv_ref[d:d + 1, cols] * moved[halo:halo + C, :]
            acc = term if acc is None else acc + term
        return _silu(acc)

    def local_prepare(chunk_ids):
        chains, a_mats, rhss = [], [], []
        for c in chunk_ids:
            q_ref, k_ref, v_ref, kt_ref, sr = source(c)
            r = scratch_row(c)
            gb = gb_ref[pl.ds(r, C), :]
            gbt = gbt_ref[:, pl.ds(r, C)]
            for j in range(hb):
                cols = slice(j * LANES, (j + 1) * LANES)
                qn = q_ref[pl.ds(sr, C), cols]
                kn = k_ref[pl.ds(sr, C), cols]
                vn = conv_v(c, v_ref, sr, cols)
                knt = kt_ref[cols, pl.ds(sr, C)].astype(BF16)
                kk = _dot(kn.astype(BF16), knt)
                qk = _dot(qn.astype(BF16), knt)
                for dr in range(2):
                    gl_, bl_ = dr * hb + j, 2 * hb + dr * hb + j
                    gcol, grow, bcol = gb[:, gl_:gl_ + 1], gbt[gl_:gl_ + 1, :], gb[:, bl_:bl_ + 1]
                    incl = (row >= col) if dr == 0 else (row <= col)
                    strict = (row > col) if dr == 0 else (row < col)
                    dec = jnp.where(incl, jnp.exp(jnp.where(incl, gcol - grow, 0.0)), 0.0)
                    qkm_ref[dr, pl.ds(r, C), cols] = (qk * dec).astype(BF16)
                    chains.append((dr, r, cols))
                    a_mats.append(jnp.where(strict, kk * dec * bcol, 0.0))
                    rhss.append(jnp.concatenate([vn * bcol, kn * (bcol * jnp.exp(gcol))], axis=1).astype(BF16))
        return chains, a_mats, rhss

    def local_solve(chains, a_mats, rhss):
        tinvs = _unit_tri_inverse(a_mats)
        uws = [_dot(tinv, rhs) for tinv, rhs in zip(tinvs, rhss)]
        for (dr, r, cols), uw in zip(chains, uws):
            u_ref[dr, pl.ds(r, C), cols] = uw[:, :LANES]
            w_ref[dr, pl.ds(r, C), cols] = uw[:, LANES:].astype(BF16)

    per_iter = max(p for p in range(1, DN_LOCAL_CHUNKS_PER_ITER + 1) if n_chunks % p == 0)
    n_groups = n_chunks // per_iter
    prepared = local_prepare(list(range(per_iter)))
    for grp in range(n_groups):
        upcoming = local_prepare(list(range((grp + 1) * per_iter, (grp + 2) * per_iter))) if grp + 1 < n_groups else None
        local_solve(*prepared)
        prepared = upcoming

    st_ref[...] = jnp.zeros(st_ref.shape, F32)
    o_ref[...] = jnp.zeros((seq, width), F32)

    def scan_step(s):
        chains = []
        for dr in range(2):
            if dr == 0:
                c = s
            elif isinstance(s, int):
                c = n_ctx - 1 - s if s < n_ctx else n_chunks - 1 - s + n_ctx
            else:
                c = n_chunks - 1 - s + n_ctx
            q_ref, _, _, kt_ref, sr = source(c)
            r = scratch_row(c)
            gb = gb_ref[pl.ds(r, C), :]
            gbt = gbt_ref[:, pl.ds(r, C)]
            for j in range(hb):
                cols = slice(j * LANES, (j + 1) * LANES)
                gl_ = dr * hb + j
                gcol, grow = gb[:, gl_:gl_ + 1], gbt[gl_:gl_ + 1, :]
                g_end_c = gcol[C - 1:C, :] if dr == 0 else gcol[0:1, :]
                g_end_r = grow[:, C - 1:C] if dr == 0 else grow[:, 0:1]
                qd = (q_ref[pl.ds(sr, C), cols] * jnp.exp(gcol)).astype(BF16)
                kdt = (kt_ref[cols, pl.ds(sr, C)] * jnp.exp(g_end_r - grow)).astype(BF16)
                chains.append((dr, j, r, cols, qd, kdt, jnp.exp(g_end_c)))
        states = [st_ref[dr * hb + j] for dr, j, *_ in chains]
        sb = [st.astype(BF16) for st in states]
        wss = [_dot(w_ref[dr, pl.ds(r, C), cols], s16)
               for (dr, j, r, cols, qd, kdt, gt), s16 in zip(chains, sb)]
        nus = [(u_ref[dr, pl.ds(r, C), cols] - ws).astype(BF16)
               for (dr, j, r, cols, qd, kdt, gt), ws in zip(chains, wss)]
        upds = [_dot(kdt, nu) for (dr, j, r, cols, qd, kdt, gt), nu in zip(chains, nus)]
        outs = [_dot(qd, s16) + _dot(qkm_ref[dr, pl.ds(r, C), cols], nu)
                for (dr, j, r, cols, qd, kdt, gt), s16, nu in zip(chains, sb, nus)]
        for (dr, j, r, cols, qd, kdt, gt), st, upd, out in zip(chains, states, upds, outs):
            st_ref[dr * hb + j] = st * gt + upd
            o_ref[pl.ds(r, C), cols] += out

    for s in range(n_ctx):
        scan_step(s)

    def scan_body(s, carry):
        scan_step(s)
        return carry

    lax.fori_loop(n_ctx, n_chunks, scan_body, 0, unroll=4)

    onorm = onorm_ref[...]
    tr = 256
    for r0 in range(0, seq, tr):
        for j in range(hb):
            cols = slice(j * LANES, (j + 1) * LANES)
            o = o_ref[r0:r0 + tr, cols]
            y = o * lax.rsqrt(jnp.mean(o * o, axis=-1, keepdims=True) + NORM_EPS) * onorm
            if r0 < ctx_len:
                yc_ref[r0:r0 + tr, cols] = (y * _silu(zc_ref[r0:r0 + tr, cols])).astype(BF16)
            else:
                q0 = r0 - ctx_len
                yl_ref[q0:q0 + tr, cols] = (y * _silu(zl_ref[q0:q0 + tr, cols])).astype(BF16)


def _dn_core(qkv_c, qkv_l, kt_c, kt_l, z_c, z_l, g_c, g_l, conv_w, alog_g, dtb_g, onorm, batch, ctx_len, lat_len):
    hb = DN_HEADS_PER_STEP
    width = hb * LANES
    dn_width = z_c.shape[1]
    ng = dn_width // width
    seq = ctx_len + lat_len
    assert ctx_len % 256 == 0 and lat_len % 256 == 0 and ctx_len % DN_CHUNK == 0

    def col_spec(rows, off):
        return pl.BlockSpec((rows, width), lambda b, g: (b, off + g))

    in_specs = [
        col_spec(ctx_len, 0), col_spec(ctx_len, ng), col_spec(ctx_len, 2 * ng),
        col_spec(lat_len, 0), col_spec(lat_len, ng), col_spec(lat_len, 2 * ng),
        pl.BlockSpec((width, ctx_len), lambda b, g: (g, b)),
        pl.BlockSpec((width, lat_len), lambda b, g: (g, b)),
        col_spec(ctx_len, 0), col_spec(lat_len, 0),
        pl.BlockSpec((ctx_len, LANES), lambda b, g: (b, g)),
        pl.BlockSpec((lat_len, LANES), lambda b, g: (b, g)),
        pl.BlockSpec((SUBLANES, width), lambda b, g: (0, 2 * ng + g)),
        pl.BlockSpec((None, SUBLANES, LANES), lambda b, g: (g, 0, 0)),
        pl.BlockSpec((None, SUBLANES, LANES), lambda b, g: (g, 0, 0)),
        pl.BlockSpec((1, LANES), lambda b, g: (0, 0)),
    ]
    scratch = [
        pltpu.VMEM((seq, LANES), F32),
        pltpu.VMEM((SUBLANES, seq), F32),
        pltpu.VMEM((2, seq, width), F32),
        pltpu.VMEM((2, seq, width), BF16),
        pltpu.VMEM((2, seq, width), BF16),
        pltpu.VMEM((seq, width), F32),
        pltpu.VMEM((2 * hb, DN_HEAD_DIM, DN_HEAD_DIM), F32),
    ]
    return pl.pallas_call(
        functools.partial(_dn_core_kernel, hb=hb, ctx_len=ctx_len, lat_len=lat_len),
        grid=(batch, ng),
        in_specs=in_specs,
        out_specs=[col_spec(ctx_len, 0), col_spec(lat_len, 0)],
        out_shape=[jax.ShapeDtypeStruct((batch * ctx_len, dn_width), BF16),
                   jax.ShapeDtypeStruct((batch * lat_len, dn_width), BF16)],
        scratch_shapes=scratch,
        compiler_params=_cparams(("arbitrary", "arbitrary")),
        name="dn_core",
    )(qkv_c, qkv_c, qkv_c, qkv_l, qkv_l, qkv_l, kt_c, kt_l, z_c, z_l, g_c, g_l, conv_w, alog_g, dtb_g, onorm)


def _da_proj_kernel(*refs, rope, qk_width, v_width):
    if rope:
        x_ref, nw_ref, mod_ref, w_ref, cos_ref, sin_ref, q_ref, k_ref, v_ref = refs
    else:
        x_ref, nw_ref, mod_ref, w_ref, q_ref, k_ref, v_ref = refs
    h = _norm_mod(x_ref[...], nw_ref[...], mod_ref[...], 0, 1).astype(BF16)
    tm = h.shape[0]
    if rope:
        cos, sin = cos_ref[...], sin_ref[...]
        lane = lax.broadcasted_iota(jnp.int32, (tm, LANES), 1)
        first_half = (lane % DA_HEAD_DIM) < (DA_HEAD_DIM // 2)
    tn = 512
    for which, dst in ((0, q_ref), (1, k_ref)):
        for j0 in range(0, qk_width, tn):
            wide = _dot(h, w_ref[:, which * qk_width + j0:which * qk_width + j0 + tn])
            for j in range(0, tn, LANES):
                y = wide[:, j:j + LANES]
                if which == 0:
                    y = y * (DA_HEAD_DIM ** -0.5 * LOG2_E)
                if rope:
                    half = DA_HEAD_DIM // 2
                    partner = jnp.where(first_half, pltpu.roll(y, LANES - half, 1), pltpu.roll(y, half, 1))
                    y = y * cos + partner * sin
                dst[:, j0 + j:j0 + j + LANES] = y.astype(BF16)
    for j in range(0, v_width, 512):
        v_ref[:, j:j + 512] = _dot(h, w_ref[:, 2 * qk_width + j:2 * qk_width + j + 512]).astype(BF16)


def _da_proj(x2d, rows_per_batch, mod_row, nw, mods_l, w_qkv, rope_tabs, qk_width, v_width):
    r, d = x2d.shape
    tm = _row_tile(rows_per_batch)
    per = rows_per_batch // tm
    in_specs = [
        pl.BlockSpec((tm, d), lambda i: (i, 0)),
        _const_spec((1, d)),
        pl.BlockSpec((None, 6, d), lambda i: (mod_row(i // per), 0, 0)),
        _const_spec(w_qkv.shape),
    ]
    args = [x2d, nw, mods_l, w_qkv]
    if rope_tabs is not None:
        in_specs += [pl.BlockSpec((tm, LANES), lambda i: (i % per, 0))] * 2
        args += list(rope_tabs)
    return pl.pallas_call(
        functools.partial(_da_proj_kernel, rope=rope_tabs is not None, qk_width=qk_width, v_width=v_width),
        grid=(r // tm,),
        in_specs=in_specs,
        out_specs=[pl.BlockSpec((tm, qk_width), lambda i: (i, 0)),
                   pl.BlockSpec((tm, qk_width), lambda i: (i, 0)),
                   pl.BlockSpec((tm, v_width), lambda i: (i, 0))],
        out_shape=[jax.ShapeDtypeStruct((r, qk_width), BF16),
                   jax.ShapeDtypeStruct((r, qk_width), BF16),
                   jax.ShapeDtypeStruct((r, v_width), BF16)],
        compiler_params=_cparams(("arbitrary",)),
        name="da_proj",
    )(*args)


def _rope_tables(n_tokens):
    rows = n_tokens // GRID_W
    row = jnp.repeat(jnp.arange(rows, dtype=jnp.int32), GRID_W).astype(F32)
    col = jnp.tile(jnp.arange(GRID_W, dtype=jnp.int32), rows).astype(F32)
    n_freq = DA_HEAD_DIM // 4
    inv = ROPE_THETA ** (-jnp.arange(n_freq, dtype=F32) / n_freq)
    ang = jnp.concatenate([row[:, None] * inv, col[:, None] * inv], axis=-1)
    cos, sin = jnp.cos(ang), jnp.sin(ang)
    reps = LANES // DA_HEAD_DIM
    return (jnp.tile(jnp.concatenate([cos, cos], axis=-1), (1, reps)),
            jnp.tile(jnp.concatenate([-sin, sin], axis=-1), (1, reps)))


def _flash_kernel(*refs, n_seg, seg_lens, tk, sub_rows, lambda_init):
    q_ref = refs[0]
    k_refs = refs[1:1 + n_seg]
    v_refs = refs[1 + n_seg:1 + 2 * n_seg]
    lam_ref, subln_ref, y_ref, s_ref, mx_ref, sum_ref = refs[1 + 2 * n_seg:]
    tq = q_ref.shape[0]
    lam = lam_ref[...]
    lam_full = (jnp.exp(jnp.sum(lam[0:1, :] * lam[1:2, :], axis=-1, keepdims=True))
                - jnp.exp(jnp.sum(lam[2:3, :] * lam[3:4, :], axis=-1, keepdims=True)) + lambda_init)
    lane = lax.broadcasted_iota(jnp.int32, (1, LANES), 1)
    map_masks = (lane < DA_HEAD_DIM, lane >= DA_HEAD_DIM)
    tiles = [(sg, r) for sg in range(n_seg) for r in range(0, seg_lens[sg], tk)]
    halves = [slice(h * LANES, (h + 1) * LANES) for h in range(tk // LANES)]
    sub = min(sub_rows, tq)

    def pass_a(rows):
        q = q_ref[rows, :]
        for mp in range(2):
            qz = jnp.where(map_masks[mp], q, jnp.zeros_like(q))
            run_max = None
            for t, (sg, r) in enumerate(tiles):
                k = k_refs[sg][r:r + tk, :]
                s = lax.dot_general(qz, k, (((1,), (1,)), ((), ())), preferred_element_type=F32)
                s_ref[mp, t, rows, :] = s
                for hs in halves:
                    run_max = s[:, hs] if run_max is None else jnp.maximum(run_max, s[:, hs])
            mx_ref[mp, rows, :] = jnp.broadcast_to(jnp.max(run_max, axis=-1, keepdims=True), (sub, LANES))

    def pass_b(rows):
        rb = 64
        for mp in range(2):
            for r0 in range(rows.start, rows.stop, rb):
                blk = slice(r0, r0 + rb)
                row_max = jnp.concatenate([mx_ref[mp, blk, :]] * len(halves), axis=1)
                run_sum = None
                for t in range(len(tiles)):
                    e = jnp.exp2(s_ref[mp, t, blk, :] - row_max)
                    s_ref[mp, t, blk, :] = e
                    for hs in halves:
                        run_sum = e[:, hs] if run_sum is None else run_sum + e[:, hs]
                total = jnp.sum(run_sum, axis=-1, keepdims=True)
                coef = (1.0 / total) if mp == 0 else (lam_full / total)
                sum_ref[mp, blk, :] = jnp.broadcast_to(coef, (rb, LANES))

    def pass_c(rows):
        c0 = jnp.concatenate([sum_ref[0, rows, :]] * len(halves), axis=1)
        c1 = jnp.concatenate([sum_ref[1, rows, :]] * len(halves), axis=1)
        acc = None
        for t, (sg, r) in enumerate(tiles):
            p = (s_ref[0, t, rows, :] * c0 - s_ref[1, t, rows, :] * c1).astype(BF16)
            part = _dot(p, v_refs[sg][r:r + tk, :])
            acc = part if acc is None else acc + part
        y = acc * lax.rsqrt(jnp.mean(acc * acc, axis=-1, keepdims=True) + NORM_EPS) * subln_ref[...]
        y_ref[rows, :] = (y * (1.0 - lambda_init)).astype(BF16)

    subs = [slice(r0, r0 + sub) for r0 in range(0, tq, sub)]
    pass_a(subs[0])
    for u in range(1, len(subs)):
        pass_a(subs[u])
        pass_b(subs[u - 1])
        pass_c(subs[u - 1])
    pass_b(subs[-1])
    pass_c(subs[-1])


def _flash(q, ks, vs, seg_lens, q_rows_per_batch, lam, subln, lambda_init, batch, heads):
    tq = min(2048, q_rows_per_batch)
    tk = 256
    nq = q_rows_per_batch // tq
    n_seg = len(ks)
    n_tiles = sum(n // tk for n in seg_lens)
    assert all(n % tk == 0 for n in seg_lens)
    in_specs = [pl.BlockSpec((tq, LANES), lambda b, h, i: (b * nq + i, h))]
    for n in list(seg_lens) * 2:
        in_specs.append(pl.BlockSpec((n, LANES), lambda b, h, i: (b, h)))
    in_specs += [pl.BlockSpec(lam.shape, lambda b, h, i: (0, 0)), pl.BlockSpec((1, LANES), lambda b, h, i: (0, 0))]
    return pl.pallas_call(
        functools.partial(_flash_kernel, n_seg=n_seg, seg_lens=tuple(seg_lens), tk=tk, sub_rows=256,
                          lambda_init=lambda_init),
        grid=(batch, heads, nq),
        in_specs=in_specs,
        out_specs=pl.BlockSpec((tq, LANES), lambda b, h, i: (b * nq + i, h)),
        out_shape=jax.ShapeDtypeStruct((q.shape[0], heads * DA_V_DIM), BF16),
        scratch_shapes=[pltpu.VMEM((2, n_tiles, tq, tk), F32),
                        pltpu.VMEM((2, tq, LANES), F32),
                        pltpu.VMEM((2, tq, LANES), F32)],
        compiler_params=_cparams(("arbitrary", "arbitrary", "arbitrary")),
        name="diff_flash",
    )(q, *ks, *vs, lam, subln)


def _post_kernel(*refs, final, ff_tile):
    if final:
        x_ref, y_ref, mod_ref, wo_ref, nw_ref, w1_ref, w2_ref, fw_ref, o_ref, acc_ref = refs
    else:
        x_ref, y_ref, mod_ref, wo_ref, nw_ref, w1_ref, w2_ref, o_ref, acc_ref = refs
    mod = mod_ref[...]
    x1 = x_ref[...] + mod[2:3, :] * _dot(y_ref[...], wo_ref[...])
    h = _norm_mod(x1, nw_ref[...], mod, 3, 4).astype(BF16)
    d_ff = w1_ref.shape[1]
    for f in range(0, d_ff, ff_tile):
        a = jnp.maximum(_dot(h, w1_ref[:, f:f + ff_tile]), 0.0)
        part = _dot((a * a).astype(BF16), w2_ref[f:f + ff_tile, :])
        if f == 0:
            acc_ref[...] = part
        else:
            acc_ref[...] += part
    x2 = x1 + mod[5:6, :] * acc_ref[...]
    if final:
        x2 = x2 * lax.rsqrt(jnp.mean(x2 * x2, axis=-1, keepdims=True) + NORM_EPS) * fw_ref[...]
    o_ref[...] = x2


def _post(x2d, y2d, rows_per_batch, mod_row, mods_l, w_out, nw, w1, w2, final_w):
    r, d = x2d.shape
    tm = _row_tile(rows_per_batch)
    per = rows_per_batch // tm
    final = final_w is not None
    in_specs = [
        pl.BlockSpec((tm, d), lambda i: (i, 0)),
        pl.BlockSpec((tm, y2d.shape[1]), lambda i: (i, 0)),
        pl.BlockSpec((None, 6, d), lambda i: (mod_row(i // per), 0, 0)),
        _const_spec(w_out.shape),
        _const_spec((1, d)),
        _const_spec(w1.shape),
        _const_spec(w2.shape),
    ]
    args = [x2d, y2d, mods_l, w_out, nw, w1, w2]
    if final:
        in_specs.append(_const_spec((1, d)))
        args.append(final_w)
    return pl.pallas_call(
        functools.partial(_post_kernel, final=final, ff_tile=1024),
        grid=(r // tm,),
        in_specs=in_specs,
        out_specs=pl.BlockSpec((tm, d), lambda i: (i, 0)),
        out_shape=jax.ShapeDtypeStruct((r, d), F32),
        scratch_shapes=[pltpu.VMEM((tm, d), F32)],
        compiler_params=_cparams(("arbitrary",)),
        name="post_mlp",
    )(*args)


def _dn_gate_layout(n_heads, hb):
    ng = n_heads // hb
    idx = np.full((ng, LANES), -1, np.int64)
    for g in range(ng):
        for kind in range(4):
            for j in range(hb):
                idx[g, kind * hb + j] = kind * n_heads + g * hb + j
    return idx.reshape(-1)


def kernel(x, c, ctx, c_ctx, ada_w, ada_b, norm_w, mlp_w1, mlp_w2, dn_w_in, dn_conv, dn_a_log, dn_dt_bias,
           dn_out_norm, dn_w_out, da_w_qkv, da_lambda, da_subln, da_w_out, final_norm):
    batch, lat_len, d = x.shape
    ctx_len = ctx.shape[1]
    depth = ada_w.shape[0]
    dn_width = dn_w_out.shape[1]
    dn_heads = dn_width // DN_HEAD_DIM
    da_v_width = da_w_out.shape[1]
    da_heads = da_v_width // DA_V_DIM
    da_qk_width = da_heads * 2 * DA_HEAD_DIM
    hb = DN_HEADS_PER_STEP

    n_rows = -(-(batch + 1) // SUBLANES) * SUBLANES
    cc = jnp.zeros((n_rows, d), F32).at[:batch].set(c).at[batch].set(c_ctx)
    mods = _modulation(cc, ada_w, ada_b).reshape(depth, n_rows, 6, d)
    lat_row = lambda b: b
    ctx_row = lambda b: batch

    xl = x.reshape(batch * lat_len, d)
    xc = ctx.reshape(batch * ctx_len, d)

    gate_idx = _dn_gate_layout(dn_heads, hb)
    gate_valid = jnp.asarray(gate_idx >= 0)
    gate_src = jnp.asarray(np.maximum(gate_idx, 0))
    ng = dn_heads // hb
    rope_tabs = _rope_tables(lat_len)

    for i in range(depth):
        last = i == depth - 1
        mods_l = mods[i]
        j = i // 2
        if i % 2 == 0:
            w_in = dn_w_in[j]
            w_main = w_in[:, :4 * dn_width].astype(BF16)
            w_gate = jnp.where(gate_valid[None, :], jnp.take(w_in[:, 4 * dn_width:], gate_src, axis=1), 0.0).astype(BF16)
            conv_w = jnp.zeros((SUBLANES, 3 * dn_width), F32).at[:DN_CONV_W].set(dn_conv[j])
            def gate_rows(p):
                rows = jnp.transpose(p.reshape(2, ng, hb), (1, 0, 2)).reshape(ng, 2 * hb, 1)
                return jnp.zeros((ng, SUBLANES, LANES), F32).at[:, :2 * hb, :].set(
                    jnp.broadcast_to(rows, (ng, 2 * hb, LANES)))

            alog_g, dtb_g = gate_rows(dn_a_log[j]), gate_rows(dn_dt_bias[j])
            nw0 = norm_w[i, 0].reshape(1, d)
            qkv_l, kt_l, z_l, g_l = _dn_proj(xl, lat_len, lat_row, nw0, mods_l, w_main, w_gate, conv_w)
            qkv_c, kt_c, z_c, g_c = _dn_proj(xc, ctx_len, ctx_row, nw0, mods_l, w_main, w_gate, conv_w)
            y_c, y_l = _dn_core(qkv_c, qkv_l, kt_c, kt_l, z_c, z_l, g_c, g_l, conv_w, alog_g, dtb_g,
                                dn_out_norm[j].reshape(1, DN_HEAD_DIM), batch, ctx_len, lat_len)
            w_out = dn_w_out[j].astype(BF16)
        else:
            lambda_init = 0.8 - 0.6 * math.exp(-0.3 * i)
            w_qkv = da_w_qkv[j].astype(BF16)
            nw0 = norm_w[i, 0].reshape(1, d)
            q_l, k_l, v_l = _da_proj(xl, lat_len, lat_row, nw0, mods_l, w_qkv, rope_tabs, da_qk_width, da_v_width)
            q_c, k_c, v_c = _da_proj(xc, ctx_len, ctx_row, nw0, mods_l, w_qkv, None, da_qk_width, da_v_width)
            lam = da_lambda[j]
            subln = da_subln[j].reshape(1, DA_V_DIM)
            y_l = _flash(q_l, (k_c, k_l), (v_c, v_l), (ctx_len, lat_len), lat_len, lam, subln, lambda_init,
                         batch, da_heads)
            y_c = None if last else _flash(q_c, (k_c,), (v_c,), (ctx_len,), ctx_len, lam, subln, lambda_init,
                                           batch, da_heads)
            w_out = da_w_out[j].astype(BF16)
        nw1 = norm_w[i, 1].reshape(1, d)
        w1 = mlp_w1[i].astype(BF16)
        w2 = mlp_w2[i].astype(BF16)
        xl = _post(xl, y_l, lat_len, lat_row, mods_l, w_out, nw1, w1, w2,
                   final_norm.reshape(1, d) if last else None)
        if not last:
            xc = _post(xc, y_c, ctx_len, ctx_row, mods_l, w_out, nw1, w1, w2, None)
    return xl.reshape(batch, lat_len, d)
```

```python
import functools
import math

import jax
import jax.numpy as jnp
import numpy as np
from jax import lax
from jax.experimental import pallas as pl
from jax.experimental.pallas import tpu as pltpu

F32 = jnp.float32
BF16 = jnp.bfloat16

NORM_EPS = 1e-6
ROPE_THETA = 10000.0
GRID_W = 64
LANES = 128
SUBLANES = 8
VMEM_LIMIT = 56 * 1024 * 1024

DN_HEAD_DIM = 128
DN_CONV_W = 5
DN_CHUNK = 128
DN_BASE = 16
DN_HEADS_PER_STEP = 2
DN_LOCAL_CHUNKS_PER_ITER = 3
DA_HEAD_DIM = 64
DA_V_DIM = 128
LOG2_E = 1.4426950408889634


def _dot(a, b):
    return jnp.dot(a, b, preferred_element_type=F32)


def _mm(a, b):
    return jnp.dot(a.astype(BF16), b.astype(BF16), preferred_element_type=F32)


def _silu(x):
    return x * jax.nn.sigmoid(x)


def _cparams(sem):
    return pltpu.CompilerParams(dimension_semantics=sem, vmem_limit_bytes=VMEM_LIMIT)


def _const_spec(shape):
    nd = len(shape)
    return pl.BlockSpec(shape, lambda *_: (0,) * nd, pipeline_mode=pl.Buffered(1))


def _mod_kernel(c_ref, w_ref, b_ref, o_ref):
    s = _silu(c_ref[...]).astype(BF16)
    o_ref[...] = _dot(s, w_ref[...].astype(BF16)) + b_ref[...]


def _modulation(cc, ada_w, ada_b):
    depth, d, n6 = ada_w.shape
    r = cc.shape[0]
    tn = n6 // 4
    return pl.pallas_call(
        _mod_kernel,
        grid=(depth, n6 // tn),
        in_specs=[
            pl.BlockSpec((r, d), lambda l, j: (0, 0)),
            pl.BlockSpec((None, d, tn), lambda l, j: (l, 0, j)),
            pl.BlockSpec((None, 1, tn), lambda l, j: (l, 0, j)),
        ],
        out_specs=pl.BlockSpec((None, r, tn), lambda l, j: (l, 0, j)),
        out_shape=jax.ShapeDtypeStruct((depth, r, n6), F32),
        compiler_params=_cparams(("arbitrary", "arbitrary")),
        name="adaln_mod",
    )(cc, ada_w, ada_b.reshape(depth, 1, n6))


def _norm_mod(x, nw, mod, shift_idx, scale_idx):
    y = x * lax.rsqrt(jnp.mean(x * x, axis=-1, keepdims=True) + NORM_EPS) * nw
    return y * (1.0 + mod[scale_idx:scale_idx + 1, :]) + mod[shift_idx:shift_idx + 1, :]


def _row_tile(rows_per_batch):
    tm = min(512, rows_per_batch)
    assert rows_per_batch % tm == 0
    return tm


DN_PROJ_HALO = 16


def _dn_proj_kernel(x_ref, xt_ref, xb_ref, nw_ref, mod_ref, wm_ref, wg_ref, cw_ref,
                    qkv_ref, kt_ref, z_ref, g_ref, *, width, per):
    halo = DN_PROJ_HALO
    tm = x_ref.shape[0]
    pos = pl.program_id(0) % per
    first, last = pos == 0, pos == per - 1
    xx = jnp.concatenate([xt_ref[...], x_ref[...], xb_ref[...]], axis=0)
    hh = _norm_mod(xx, nw_ref[...], mod_ref[...], 0, 1).astype(BF16)
    h = hh[halo:halo + tm, :]
    tn = 512
    for j0 in range(0, 2 * width, tn):
        wide = _dot(hh, wm_ref[:, j0:j0 + tn])
        wide = jnp.concatenate([jnp.where(first, 0.0, wide[0:halo, :]), wide[halo:halo + tm, :],
                                jnp.where(last, 0.0, wide[halo + tm:, :])], axis=0)
        for j in range(j0, j0 + tn, LANES):
            win = wide[:, j - j0:j - j0 + LANES]
            acc = None
            for d in range(DN_CONV_W):
                shift = d - DN_CONV_W // 2
                moved = win if shift == 0 else pltpu.roll(win, (-shift) % win.shape[0], 0)
                term = cw_ref[d:d + 1, j:j + LANES] * moved[halo:halo + tm, :]
                acc = term if acc is None else acc + term
            y = _silu(acc)
            y = y * lax.rsqrt(jnp.sum(y * y, axis=-1, keepdims=True) + 1e-6)
            if j < width:
                y = y * (DN_HEAD_DIM ** -0.5)
            qkv_ref[:, j:j + LANES] = y
            if j >= width:
                kt_ref[j - width:j - width + LANES, :] = y.T
    for j in range(2 * width, 3 * width, tn):
        qkv_ref[:, j:j + tn] = _dot(h, wm_ref[:, j:j + tn])
    for j in range(0, width, tn):
        z_ref[:, j:j + tn] = _dot(h, wm_ref[:, 3 * width + j:3 * width + j + tn])
    g_ref[...] = _dot(h, wg_ref[...])


def _dn_proj(x2d, rows_per_batch, mod_row, nw, mods_l, w_main, w_gate, conv_w):
    r, d = x2d.shape
    width = w_main.shape[1] // 4
    gw = w_gate.shape[1]
    tm = _row_tile(rows_per_batch)
    per = rows_per_batch // tm
    halo = DN_PROJ_HALO
    hpt = tm // halo
    return pl.pallas_call(
        functools.partial(_dn_proj_kernel, width=width, per=per),
        grid=(r // tm,),
        in_specs=[
            pl.BlockSpec((tm, d), lambda i: (i, 0)),
            pl.BlockSpec((halo, d), lambda i: (jnp.maximum(i * hpt - 1, 0), 0)),
            pl.BlockSpec((halo, d), lambda i: (jnp.minimum((i + 1) * hpt, r // halo - 1), 0)),
            _const_spec((1, d)),
            pl.BlockSpec((None, 6, d), lambda i: (mod_row(i // per), 0, 0)),
            _const_spec(w_main.shape),
            _const_spec(w_gate.shape),
            _const_spec(conv_w.shape),
        ],
        out_specs=[
            pl.BlockSpec((tm, 3 * width), lambda i: (i, 0)),
            pl.BlockSpec((width, tm), lambda i: (0, i)),
            pl.BlockSpec((tm, width), lambda i: (i, 0)),
            pl.BlockSpec((tm, gw), lambda i: (i, 0)),
        ],
        out_shape=[
            jax.ShapeDtypeStruct((r, 3 * width), F32),
            jax.ShapeDtypeStruct((width, r), F32),
            jax.ShapeDtypeStruct((r, width), F32),
            jax.ShapeDtypeStruct((r, gw), F32),
        ],
        compiler_params=_cparams(("arbitrary",)),
        name="dn_proj",
    )(x2d, x2d, x2d, nw, mods_l, w_main, w_gate, conv_w)


def _cumsum_lanes(x, reverse):
    n = x.shape[1]
    lane = lax.broadcasted_iota(jnp.int32, x.shape, 1)
    s = 1
    while s < n:
        if reverse:
            x = x + jnp.where(lane < n - s, pltpu.roll(x, n - s, 1), 0.0)
        else:
            x = x + jnp.where(lane >= s, pltpu.roll(x, s, 1), 0.0)
        s *= 2
    return x


def _unit_tri_inverse(mats):
    n = mats[0].shape[0]
    row = lax.broadcasted_iota(jnp.int32, (n, n), 0)
    col = lax.broadcasted_iota(jnp.int32, (n, n), 1)

    def same_block(size):
        return (row // size) == (col // size)

    base = same_block(DN_BASE)
    negs = [-a for a in mats]
    accs = [jnp.where(base, na, 0.0) for na in negs]
    steps = int(math.log2(DN_BASE))
    ps = [_mm(p, p) for p in accs]
    for k in range(1, steps):
        if k + 1 < steps:
            both = [_mm(p, jnp.concatenate([acc, p], axis=1)) for acc, p in zip(accs, ps)]
            accs = [acc + p + b[:, :n] for acc, p, b in zip(accs, ps, both)]
            ps = [b[:, n:] for b in both]
        else:
            accs = [acc + p + _mm(p, acc) for acc, p in zip(accs, ps)]
    xs = [jnp.where(row == col, 1.0, acc).astype(BF16) for acc in accs]
    size = DN_BASE
    while size < n:
        sel = same_block(2 * size) & jnp.logical_not(same_block(size))
        offs = [jnp.where(sel, na, 0.0).astype(BF16) for na in negs]
        ys = [_dot(x, off).astype(BF16) for x, off in zip(xs, offs)]
        xs = [x + _dot(y, x).astype(BF16) for x, y in zip(xs, ys)]
        size *= 2
    return xs


def _dn_core_kernel(qc_ref, kc_ref, vc_ref, ql_ref, kl_ref, vl_ref, ktc_ref, ktl_ref, zc_ref, zl_ref, gc_ref, gl_ref,
                    cwv_ref, alog_ref, dtb_ref, onorm_ref,
                    yc_ref, yl_ref,
                    gb_ref, gbt_ref, u_ref, w_ref, qkm_ref, o_ref, st_ref, *, hb, ctx_len, lat_len):
    C = DN_CHUNK
    seq = ctx_len + lat_len
    n_ctx, n_chunks = ctx_len // C, seq // C
    width = hb * LANES

    assert 4 * hb <= SUBLANES
    raws = []
    for c in range(n_chunks):
        src = gc_ref[c * C:(c + 1) * C, :] if c < n_ctx else gl_ref[(c - n_ctx) * C:(c - n_ctx + 1) * C, :]
        raws.append(src.T[0:SUBLANES, :])
    raw = jnp.concatenate(raws, axis=0)
    grow_id = lax.broadcasted_iota(jnp.int32, raw.shape, 0) % SUBLANES
    a = raw + jnp.concatenate([dtb_ref[...]] * n_chunks, axis=0)
    g = (jnp.concatenate([-jnp.exp(alog_ref[...])] * n_chunks, axis=0)
         * (jnp.maximum(a, 0.0) + jnp.log1p(jnp.exp(-jnp.abs(a)))))
    pre = _cumsum_lanes(g, reverse=False)
    suf = _cumsum_lanes(g, reverse=True)
    vals = jnp.where(grow_id < hb, pre, jnp.where(grow_id < 2 * hb, suf, jax.nn.sigmoid(raw)))
    zero_rows = jnp.zeros((C - SUBLANES, C), F32)
    for c in range(n_chunks):
        tile = vals[c * SUBLANES:(c + 1) * SUBLANES, :]
        gbt_ref[:, c * C:(c + 1) * C] = tile
        gb_ref[c * C:(c + 1) * C, :] = jnp.concatenate([tile, zero_rows], axis=0).T

    def source(c):
        if isinstance(c, int) and c < n_ctx:
            return qc_ref, kc_ref, vc_ref, ktc_ref, c * C
        r = (c - n_ctx) * C
        return ql_ref, kl_ref, vl_ref, ktl_ref, (r if isinstance(c, int) else pl.multiple_of(r, C))

    def scratch_row(c):
        return c * C if isinstance(c, int) else pl.multiple_of(c * C, C)

    row = lax.broadcasted_iota(jnp.int32, (C, C), 0)
    col = lax.broadcasted_iota(jnp.int32, (C, C), 1)

    halo = SUBLANES
    zeros_halo = jnp.zeros((halo, LANES), F32)

    def conv_v(c, v_ref, sr, cols):
        if isinstance(c, int):
            seg_len = ctx_len if c < n_ctx else lat_len
            top = zeros_halo if sr == 0 else v_ref[sr - halo:sr, cols]
            bot = zeros_halo if sr + C == seg_len else v_ref[sr + C:sr + C + halo, cols]
        else:
            top_start = pl.multiple_of(jnp.maximum(sr - halo, 0), halo)
            bot_start = pl.multiple_of(jnp.minimum(sr + C, lat_len - halo), halo)
            top = jnp.where(sr == 0, 0.0, v_ref[pl.ds(top_start, halo), cols])
            bot = jnp.where(sr + C == lat_len, 0.0, v_ref[pl.ds(bot_start, halo), cols])
        win = jnp.concatenate([top, v_ref[pl.ds(sr, C), cols], bot], axis=0)
        acc = None
        for d in range(DN_CONV_W):
            shift = d - DN_CONV_W // 2
            moved = win if shift == 0 else pltpu.roll(win, (-shift) % win.shape[0], 0)
            term = cwv_ref[d:d + 1, cols] * moved[halo:halo + C, :]
            acc = term if acc is None else acc + term
        return _silu(acc)

    def local_prepare(chunk_ids):
        chains, a_mats, rhss = [], [], []
        for c in chunk_ids:
            q_ref, k_ref, v_ref, kt_ref, sr = source(c)
            r = scratch_row(c)
            gb = gb_ref[pl.ds(r, C), :]
            gbt = gbt_ref[:, pl.ds(r, C)]
            for j in range(hb):
                cols = slice(j * LANES, (j + 1) * LANES)
                qn = q_ref[pl.ds(sr, C), cols]
                kn = k_ref[pl.ds(sr, C), cols]
                vn = conv_v(c, v_ref, sr, cols)
                knt = kt_ref[cols, pl.ds(sr, C)].astype(BF16)
                kk = _dot(kn.astype(BF16), knt)
                qk = _dot(qn.astype(BF16), knt)
                for dr in range(2):
                    gl_, bl_ = dr * hb + j, 2 * hb + dr * hb + j
                    gcol, grow, bcol = gb[:, gl_:gl_ + 1], gbt[gl_:gl_ + 1, :], gb[:, bl_:bl_ + 1]
                    incl = (row >= col) if dr == 0 else (row <= col)
                    strict = (row > col) if dr == 0 else (row < col)
                    dec = jnp.where(incl, jnp.exp(jnp.where(incl, gcol - grow, 0.0)), 0.0)
                    qkm_ref[dr, pl.ds(r, C), cols] = (qk * dec).astype(BF16)
                    chains.append((dr, r, cols))
                    a_mats.append(jnp.where(strict, kk * dec * bcol, 0.0))
                    rhss.append(jnp.concatenate([vn * bcol, kn * (bcol * jnp.exp(gcol))], axis=1).astype(BF16))
        return chains, a_mats, rhss

    def local_solve(chains, a_mats, rhss):
        tinvs = _unit_tri_inverse(a_mats)
        uws = [_dot(tinv, rhs) for tinv, rhs in zip(tinvs, rhss)]
        for (dr, r, cols), uw in zip(chains, uws):
            u_ref[dr, pl.ds(r, C), cols] = uw[:, :LANES]
            w_ref[dr, pl.ds(r, C), cols] = uw[:, LANES:].astype(BF16)

    per_iter = max(p for p in range(1, DN_LOCAL_CHUNKS_PER_ITER + 1) if n_chunks % p == 0)
    n_groups = n_chunks // per_iter
    prepared = local_prepare(list(range(per_iter)))
    for grp in range(n_groups):
        upcoming = local_prepare(list(range((grp + 1) * per_iter, (grp + 2) * per_iter))) if grp + 1 < n_groups else None
        local_solve(*prepared)
        prepared = upcoming

    st_ref[...] = jnp.zeros(st_ref.shape, F32)
    o_ref[...] = jnp.zeros((seq, width), F32)

    def scan_step(s):
        chains = []
        for dr in range(2):
            if dr == 0:
                c = s
            elif isinstance(s, int):
                c = n_ctx - 1 - s if s < n_ctx else n_chunks - 1 - s + n_ctx
            else:
                c = n_chunks - 1 - s + n_ctx
            q_ref, _, _, kt_ref, sr = source(c)
            r = scratch_row(c)
            gb = gb_ref[pl.ds(r, C), :]
            gbt = gbt_ref[:, pl.ds(r, C)]
            for j in range(hb):
                cols = slice(j * LANES, (j + 1) * LANES)
                gl_ = dr * hb + j
                gcol, grow = gb[:, gl_:gl_ + 1], gbt[gl_:gl_ + 1, :]
                g_end_c = gcol[C - 1:C, :] if dr == 0 else gcol[0:1, :]
                g_end_r = grow[:, C - 1:C] if dr == 0 else grow[:, 0:1]
                qd = (q_ref[pl.ds(sr, C), cols] * jnp.exp(gcol)).astype(BF16)
                kdt = (kt_ref[cols, pl.ds(sr, C)] * jnp.exp(g_end_r - grow)).astype(BF16)
                chains.append((dr, j, r, cols, qd, kdt, jnp.exp(g_end_c)))
        states = [st_ref[dr * hb + j] for dr, j, *_ in chains]
        sb = [st.astype(BF16) for st in states]
        wss = [_dot(w_ref[dr, pl.ds(r, C), cols], s16)
               for (dr, j, r, cols, qd, kdt, gt), s16 in zip(chains, sb)]
        nus = [(u_ref[dr, pl.ds(r, C), cols] - ws).astype(BF16)
               for (dr, j, r, cols, qd, kdt, gt), ws in zip(chains, wss)]
        upds = [_dot(kdt, nu) for (dr, j, r, cols, qd, kdt, gt), nu in zip(chains, nus)]
        outs = [_dot(qd, s16) + _dot(qkm_ref[dr, pl.ds(r, C), cols], nu)
                for (dr, j, r, cols, qd, kdt, gt), s16, nu in zip(chains, sb, nus)]
        for (dr, j, r, cols, qd, kdt, gt), st, upd, out in zip(chains, states, upds, outs):
            st_ref[dr * hb + j] = st * gt + upd
            o_ref[pl.ds(r, C), cols] += out

    for s in range(n_ctx):
        scan_step(s)

    def scan_body(s, carry):
        scan_step(s)
        return carry

    lax.fori_loop(n_ctx, n_chunks, scan_body, 0, unroll=4)

    onorm = onorm_ref[...]
    tr = 256
    for r0 in range(0, seq, tr):
        for j in range(hb):
            cols = slice(j * LANES, (j + 1) * LANES)
            o = o_ref[r0:r0 + tr, cols]
            y = o * lax.rsqrt(jnp.mean(o * o, axis=-1, keepdims=True) + NORM_EPS) * onorm
            if r0 < ctx_len:
                yc_ref[r0:r0 + tr, cols] = (y * _silu(zc_ref[r0:r0 + tr, cols])).astype(BF16)
            else:
                q0 = r0 - ctx_len
                yl_ref[q0:q0 + tr, cols] = (y * _silu(zl_ref[q0:q0 + tr, cols])).astype(BF16)


def _dn_core(qkv_c, qkv_l, kt_c, kt_l, z_c, z_l, g_c, g_l, conv_w, alog_g, dtb_g, onorm, batch, ctx_len, lat_len):
    hb = DN_HEADS_PER_STEP
    width = hb * LANES
    dn_width = z_c.shape[1]
    ng = dn_width // width
    seq = ctx_len + lat_len
    assert ctx_len % 256 == 0 and lat_len % 256 == 0 and ctx_len % DN_CHUNK == 0

    def col_spec(rows, off):
        return pl.BlockSpec((rows, width), lambda b, g: (b, off + g))

    in_specs = [
        col_spec(ctx_len, 0), col_spec(ctx_len, ng), col_spec(ctx_len, 2 * ng),
        col_spec(lat_len, 0), col_spec(lat_len, ng), col_spec(lat_len, 2 * ng),
        pl.BlockSpec((width, ctx_len), lambda b, g: (g, b)),
        pl.BlockSpec((width, lat_len), lambda b, g: (g, b)),
        col_spec(ctx_len, 0), col_spec(lat_len, 0),
        pl.BlockSpec((ctx_len, LANES), lambda b, g: (b, g)),
        pl.BlockSpec((lat_len, LANES), lambda b, g: (b, g)),
        pl.BlockSpec((SUBLANES, width), lambda b, g: (0, 2 * ng + g)),
        pl.BlockSpec((None, SUBLANES, LANES), lambda b, g: (g, 0, 0)),
        pl.BlockSpec((None, SUBLANES, LANES), lambda b, g: (g, 0, 0)),
        pl.BlockSpec((1, LANES), lambda b, g: (0, 0)),
    ]
    scratch = [
        pltpu.VMEM((seq, LANES), F32),
        pltpu.VMEM((SUBLANES, seq), F32),
        pltpu.VMEM((2, seq, width), F32),
        pltpu.VMEM((2, seq, width), BF16),
        pltpu.VMEM((2, seq, width), BF16),
        pltpu.VMEM((seq, width), F32),
        pltpu.VMEM((2 * hb, DN_HEAD_DIM, DN_HEAD_DIM), F32),
    ]
    return pl.pallas_call(
        functools.partial(_dn_core_kernel, hb=hb, ctx_len=ctx_len, lat_len=lat_len),
        grid=(batch, ng),
        in_specs=in_specs,
        out_specs=[col_spec(ctx_len, 0), col_spec(lat_len, 0)],
        out_shape=[jax.ShapeDtypeStruct((batch * ctx_len, dn_width), BF16),
                   jax.ShapeDtypeStruct((batch * lat_len, dn_width), BF16)],
        scratch_shapes=scratch,
        compiler_params=_cparams(("arbitrary", "arbitrary")),
        name="dn_core",
    )(qkv_c, qkv_c, qkv_c, qkv_l, qkv_l, qkv_l, kt_c, kt_l, z_c, z_l, g_c, g_l, conv_w, alog_g, dtb_g, onorm)


def _da_proj_kernel(*refs, rope, qk_width, v_width):
    if rope:
        x_ref, nw_ref, mod_ref, w_ref, cos_ref, sin_ref, q_ref, k_ref, v_ref = refs
    else:
        x_ref, nw_ref, mod_ref, w_ref, q_ref, k_ref, v_ref = refs
    h = _norm_mod(x_ref[...], nw_ref[...], mod_ref[...], 0, 1).astype(BF16)
    tm = h.shape[0]
    if rope:
        cos, sin = cos_ref[...], sin_ref[...]
        lane = lax.broadcasted_iota(jnp.int32, (tm, LANES), 1)
        first_half = (lane % DA_HEAD_DIM) < (DA_HEAD_DIM // 2)
    tn = 512
    for which, dst in ((0, q_ref), (1, k_ref)):
        for j0 in range(0, qk_width, tn):
            wide = _dot(h, w_ref[:, which * qk_width + j0:which * qk_width + j0 + tn])
            for j in range(0, tn, LANES):
                y = wide[:, j:j + LANES]
                if which == 0:
                    y = y * (DA_HEAD_DIM ** -0.5 * LOG2_E)
                if rope:
                    half = DA_HEAD_DIM // 2
                    partner = jnp.where(first_half, pltpu.roll(y, LANES - half, 1), pltpu.roll(y, half, 1))
                    y = y * cos + partner * sin
                dst[:, j0 + j:j0 + j + LANES] = y.astype(BF16)
    for j in range(0, v_width, 512):
        v_ref[:, j:j + 512] = _dot(h, w_ref[:, 2 * qk_width + j:2 * qk_width + j + 512]).astype(BF16)


def _da_proj(x2d, rows_per_batch, mod_row, nw, mods_l, w_qkv, rope_tabs, qk_width, v_width):
    r, d = x2d.shape
    tm = _row_tile(rows_per_batch)
    per = rows_per_batch // tm
    in_specs = [
        pl.BlockSpec((tm, d), lambda i: (i, 0)),
        _const_spec((1, d)),
        pl.BlockSpec((None, 6, d), lambda i: (mod_row(i // per), 0, 0)),
        _const_spec(w_qkv.shape),
    ]
    args = [x2d, nw, mods_l, w_qkv]
    if rope_tabs is not None:
        in_specs += [pl.BlockSpec((tm, LANES), lambda i: (i % per, 0))] * 2
        args += list(rope_tabs)
    return pl.pallas_call(
        functools.partial(_da_proj_kernel, rope=rope_tabs is not None, qk_width=qk_width, v_width=v_width),
        grid=(r // tm,),
        in_specs=in_specs,
        out_specs=[pl.BlockSpec((tm, qk_width), lambda i: (i, 0)),
                   pl.BlockSpec((tm, qk_width), lambda i: (i, 0)),
                   pl.BlockSpec((tm, v_width), lambda i: (i, 0))],
        out_shape=[jax.ShapeDtypeStruct((r, qk_width), BF16),
                   jax.ShapeDtypeStruct((r, qk_width), BF16),
                   jax.ShapeDtypeStruct((r, v_width), BF16)],
        compiler_params=_cparams(("arbitrary",)),
        name="da_proj",
    )(*args)


def _rope_tables(n_tokens):
    rows = n_tokens // GRID_W
    row = jnp.repeat(jnp.arange(rows, dtype=jnp.int32), GRID_W).astype(F32)
    col = jnp.tile(jnp.arange(GRID_W, dtype=jnp.int32), rows).astype(F32)
    n_freq = DA_HEAD_DIM // 4
    inv = ROPE_THETA ** (-jnp.arange(n_freq, dtype=F32) / n_freq)
    ang = jnp.concatenate([row[:, None] * inv, col[:, None] * inv], axis=-1)
    cos, sin = jnp.cos(ang), jnp.sin(ang)
    reps = LANES // DA_HEAD_DIM
    return (jnp.tile(jnp.concatenate([cos, cos], axis=-1), (1, reps)),
            jnp.tile(jnp.concatenate([-sin, sin], axis=-1), (1, reps)))


def _flash_kernel(*refs, n_seg, seg_lens, tk, sub_rows, lambda_init):
    q_ref = refs[0]
    k_refs = refs[1:1 + n_seg]
    v_refs = refs[1 + n_seg:1 + 2 * n_seg]
    lam_ref, subln_ref, y_ref, s_ref, mx_ref, sum_ref = refs[1 + 2 * n_seg:]
    tq = q_ref.shape[0]
    lam = lam_ref[...]
    lam_full = (jnp.exp(jnp.sum(lam[0:1, :] * lam[1:2, :], axis=-1, keepdims=True))
                - jnp.exp(jnp.sum(lam[2:3, :] * lam[3:4, :], axis=-1, keepdims=True)) + lambda_init)
    lane = lax.broadcasted_iota(jnp.int32, (1, LANES), 1)
    map_masks = (lane < DA_HEAD_DIM, lane >= DA_HEAD_DIM)
    tiles = [(sg, r) for sg in range(n_seg) for r in range(0, seg_lens[sg], tk)]
    halves = [slice(h * LANES, (h + 1) * LANES) for h in range(tk // LANES)]
    sub = min(sub_rows, tq)

    def pass_a(rows):
        q = q_ref[rows, :]
        for mp in range(2):
            qz = jnp.where(map_masks[mp], q, jnp.zeros_like(q))
            run_max = None
            for t, (sg, r) in enumerate(tiles):
                k = k_refs[sg][r:r + tk, :]
                s = lax.dot_general(qz, k, (((1,), (1,)), ((), ())), preferred_element_type=F32)
                s_ref[mp, t, rows, :] = s
                for hs in halves:
                    run_max = s[:, hs] if run_max is None else jnp.maximum(run_max, s[:, hs])
            mx_ref[mp, rows, :] = jnp.broadcast_to(jnp.max(run_max, axis=-1, keepdims=True), (sub, LANES))

    def pass_b(rows):
        rb = 64
        for mp in range(2):
            for r0 in range(rows.start, rows.stop, rb):
                blk = slice(r0, r0 + rb)
                row_max = jnp.concatenate([mx_ref[mp, blk, :]] * len(halves), axis=1)
                run_sum = None
                for t in range(len(tiles)):
                    e = jnp.exp2(s_ref[mp, t, blk, :] - row_max)
                    s_ref[mp, t, blk, :] = e
                    for hs in halves:
                        run_sum = e[:, hs] if run_sum is None else run_sum + e[:, hs]
                total = jnp.sum(run_sum, axis=-1, keepdims=True)
                coef = (1.0 / total) if mp == 0 else (lam_full / total)
                sum_ref[mp, blk, :] = jnp.broadcast_to(coef, (rb, LANES))

    def pass_c(rows):
        c0 = jnp.concatenate([sum_ref[0, rows, :]] * len(halves), axis=1)
        c1 = jnp.concatenate([sum_ref[1, rows, :]] * len(halves), axis=1)
        acc = None
        for t, (sg, r) in enumerate(tiles):
            p = (s_ref[0, t, rows, :] * c0 - s_ref[1, t, rows, :] * c1).astype(BF16)
            part = _dot(p, v_refs[sg][r:r + tk, :])
            acc = part if acc is None else acc + part
        y = acc * lax.rsqrt(jnp.mean(acc * acc, axis=-1, keepdims=True) + NORM_EPS) * subln_ref[...]
        y_ref[rows, :] = (y * (1.0 - lambda_init)).astype(BF16)

    subs = [slice(r0, r0 + sub) for r0 in range(0, tq, sub)]
    pass_a(subs[0])
    for u in range(1, len(subs)):
        pass_a(subs[u])
        pass_b(subs[u - 1])
        pass_c(subs[u - 1])
    pass_b(subs[-1])
    pass_c(subs[-1])


def _flash(q, ks, vs, seg_lens, q_rows_per_batch, lam, subln, lambda_init, batch, heads):
    tq = min(2048, q_rows_per_batch)
    tk = 256
    nq = q_rows_per_batch // tq
    n_seg = len(ks)
    n_tiles = sum(n // tk for n in seg_lens)
    assert all(n % tk == 0 for n in seg_lens)
    in_specs = [pl.BlockSpec((tq, LANES), lambda b, h, i: (b * nq + i, h))]
    for n in list(seg_lens) * 2:
        in_specs.append(pl.BlockSpec((n, LANES), lambda b, h, i: (b, h)))
    in_specs += [pl.BlockSpec(lam.shape, lambda b, h, i: (0, 0)), pl.BlockSpec((1, LANES), lambda b, h, i: (0, 0))]
    return pl.pallas_call(
        functools.partial(_flash_kernel, n_seg=n_seg, seg_lens=tuple(seg_lens), tk=tk, sub_rows=512,
                          lambda_init=lambda_init),
        grid=(batch, heads, nq),
        in_specs=in_specs,
        out_specs=pl.BlockSpec((tq, LANES), lambda b, h, i: (b * nq + i, h)),
        out_shape=jax.ShapeDtypeStruct((q.shape[0], heads * DA_V_DIM), BF16),
        scratch_shapes=[pltpu.VMEM((2, n_tiles, tq, tk), F32),
                        pltpu.VMEM((2, tq, LANES), F32),
                        pltpu.VMEM((2, tq, LANES), F32)],
        compiler_params=_cparams(("arbitrary", "arbitrary", "arbitrary")),
        name="diff_flash",
    )(q, *ks, *vs, lam, subln)


def _post_kernel(*refs, final, ff_tile):
    if final:
        x_ref, y_ref, mod_ref, wo_ref, nw_ref, w1_ref, w2_ref, fw_ref, o_ref, acc_ref = refs
    else:
        x_ref, y_ref, mod_ref, wo_ref, nw_ref, w1_ref, w2_ref, o_ref, acc_ref = refs
    mod = mod_ref[...]
    x1 = x_ref[...] + mod[2:3, :] * _dot(y_ref[...], wo_ref[...])
    h = _norm_mod(x1, nw_ref[...], mod, 3, 4).astype(BF16)
    d_ff = w1_ref.shape[1]
    for f in range(0, d_ff, ff_tile):
        a = jnp.maximum(_dot(h, w1_ref[:, f:f + ff_tile]), 0.0)
        part = _dot((a * a).astype(BF16), w2_ref[f:f + ff_tile, :])
        if f == 0:
            acc_ref[...] = part
        else:
            acc_ref[...] += part
    x2 = x1 + mod[5:6, :] * acc_ref[...]
    if final:
        x2 = x2 * lax.rsqrt(jnp.mean(x2 * x2, axis=-1, keepdims=True) + NORM_EPS) * fw_ref[...]
    o_ref[...] = x2


def _post(x2d, y2d, rows_per_batch, mod_row, mods_l, w_out, nw, w1, w2, final_w):
    r, d = x2d.shape
    tm = _row_tile(rows_per_batch)
    per = rows_per_batch // tm
    final = final_w is not None
    in_specs = [
        pl.BlockSpec((tm, d), lambda i: (i, 0)),
        pl.BlockSpec((tm, y2d.shape[1]), lambda i: (i, 0)),
        pl.BlockSpec((None, 6, d), lambda i: (mod_row(i // per), 0, 0)),
        _const_spec(w_out.shape),
        _const_spec((1, d)),
        _const_spec(w1.shape),
        _const_spec(w2.shape),
    ]
    args = [x2d, y2d, mods_l, w_out, nw, w1, w2]
    if final:
        in_specs.append(_const_spec((1, d)))
        args.append(final_w)
    return pl.pallas_call(
        functools.partial(_post_kernel, final=final, ff_tile=1024),
        grid=(r // tm,),
        in_specs=in_specs,
        out_specs=pl.BlockSpec((tm, d), lambda i: (i, 0)),
        out_shape=jax.ShapeDtypeStruct((r, d), F32),
        scratch_shapes=[pltpu.VMEM((tm, d), F32)],
        compiler_params=_cparams(("arbitrary",)),
        name="post_mlp",
    )(*args)


def _dn_gate_layout(n_heads, hb):
    ng = n_heads // hb
    idx = np.full((ng, LANES), -1, np.int64)
    for g in range(ng):
        for kind in range(4):
            for j in range(hb):
                idx[g, kind * hb + j] = kind * n_heads + g * hb + j
    return idx.reshape(-1)


def kernel(x, c, ctx, c_ctx, ada_w, ada_b, norm_w, mlp_w1, mlp_w2, dn_w_in, dn_conv, dn_a_log, dn_dt_bias,
           dn_out_norm, dn_w_out, da_w_qkv, da_lambda, da_subln, da_w_out, final_norm):
    batch, lat_len, d = x.shape
    ctx_len = ctx.shape[1]
    depth = ada_w.shape[0]
    dn_width = dn_w_out.shape[1]
    dn_heads = dn_width // DN_HEAD_DIM
    da_v_width = da_w_out.shape[1]
    da_heads = da_v_width // DA_V_DIM
    da_qk_width = da_heads * 2 * DA_HEAD_DIM
    hb = DN_HEADS_PER_STEP

    n_rows = -(-(batch + 1) // SUBLANES) * SUBLANES
    cc = jnp.zeros((n_rows, d), F32).at[:batch].set(c).at[batch].set(c_ctx)
    mods = _modulation(cc, ada_w, ada_b).reshape(depth, n_rows, 6, d)
    lat_row = lambda b: b
    ctx_row = lambda b: batch

    xl = x.reshape(batch * lat_len, d)
    xc = ctx.reshape(batch * ctx_len, d)

    gate_idx = _dn_gate_layout(dn_heads, hb)
    gate_valid = jnp.asarray(gate_idx >= 0)
    gate_src = jnp.asarray(np.maximum(gate_idx, 0))
    ng = dn_heads // hb
    rope_tabs = _rope_tables(lat_len)

    for i in range(depth):
        last = i == depth - 1
        mods_l = mods[i]
        j = i // 2
        if i % 2 == 0:
            w_in = dn_w_in[j]
            w_main = w_in[:, :4 * dn_width].astype(BF16)
            w_gate = jnp.where(gate_valid[None, :], jnp.take(w_in[:, 4 * dn_width:], gate_src, axis=1), 0.0).astype(BF16)
            conv_w = jnp.zeros((SUBLANES, 3 * dn_width), F32).at[:DN_CONV_W].set(dn_conv[j])
            def gate_rows(p):
                rows = jnp.transpose(p.reshape(2, ng, hb), (1, 0, 2)).reshape(ng, 2 * hb, 1)
                return jnp.zeros((ng, SUBLANES, LANES), F32).at[:, :2 * hb, :].set(
                    jnp.broadcast_to(rows, (ng, 2 * hb, LANES)))

            alog_g, dtb_g = gate_rows(dn_a_log[j]), gate_rows(dn_dt_bias[j])
            nw0 = norm_w[i, 0].reshape(1, d)
            qkv_l, kt_l, z_l, g_l = _dn_proj(xl, lat_len, lat_row, nw0, mods_l, w_main, w_gate, conv_w)
            qkv_c, kt_c, z_c, g_c = _dn_proj(xc, ctx_len, ctx_row, nw0, mods_l, w_main, w_gate, conv_w)
            y_c, y_l = _dn_core(qkv_c, qkv_l, kt_c, kt_l, z_c, z_l, g_c, g_l, conv_w, alog_g, dtb_g,
                                dn_out_norm[j].reshape(1, DN_HEAD_DIM), batch, ctx_len, lat_len)
            w_out = dn_w_out[j].astype(BF16)
        else:
            lambda_init = 0.8 - 0.6 * math.exp(-0.3 * i)
            w_qkv = da_w_qkv[j].astype(BF16)
            nw0 = norm_w[i, 0].reshape(1, d)
            q_l, k_l, v_l = _da_proj(xl, lat_len, lat_row, nw0, mods_l, w_qkv, rope_tabs, da_qk_width, da_v_width)
            q_c, k_c, v_c = _da_proj(xc, ctx_len, ctx_row, nw0, mods_l, w_qkv, None, da_qk_width, da_v_width)
            lam = da_lambda[j]
            subln = da_subln[j].reshape(1, DA_V_DIM)
            y_l = _flash(q_l, (k_c, k_l), (v_c, v_l), (ctx_len, lat_len), lat_len, lam, subln, lambda_init,
                         batch, da_heads)
            y_c = None if last else _flash(q_c, (k_c,), (v_c,), (ctx_len,), ctx_len, lam, subln, lambda_init,
                                           batch, da_heads)
            w_out = da_w_out[j].astype(BF16)
        nw1 = norm_w[i, 1].reshape(1, d)
        w1 = mlp_w1[i].astype(BF16)
        w2 = mlp_w2[i].astype(BF16)
        xl = _post(xl, y_l, lat_len, lat_row, mods_l, w_out, nw1, w1, w2,
                   final_norm.reshape(1, d) if last else None)
        if not last:
            xc = _post(xc, y_c, ctx_len, ctx_row, mods_l, w_out, nw1, w1, w2, None)
    return xl.reshape(batch, lat_len, d)
```
